```python
import math
import jax, jax.numpy as jnp
from jax import lax

D_MODEL = 2048
BATCH = 2
SEQ = 4096
DEPTH = 4

N_META = 16
N_A_LAYERS = DEPTH // 2
N_B_LAYERS = DEPTH - N_A_LAYERS
D_RNN = 5 * D_MODEL // 4
N_GATE_BLOCKS = 16
GATE_BLOCK = D_RNN // N_GATE_BLOCKS
CONV_WIDTH = 4
LRU_C = 8.0
HEAD_DIM = 128
N_DIFF_HEADS = D_MODEL // (2 * HEAD_DIM)
QK_WIDTH = 2 * N_DIFF_HEADS * HEAD_DIM
V_WIDTH = N_DIFF_HEADS * 2 * HEAD_DIM
D_FF = 4 * D_MODEL
ROPE_THETA = 10000.0
Q_BLOCK = 128
NORM_EPS = 1e-6
SUBLN_EPS = 1e-5

kernel_name = "yoco_rglru_diffattn_hybrid"


def rmsnorm(x, g, eps=NORM_EPS):
    xf = x.astype(jnp.float32)
    xf = xf * lax.rsqrt(jnp.mean(xf * xf, axis=-1, keepdims=True) + eps)
    return (xf * g.astype(jnp.float32)).astype(x.dtype)


def rope_tables(length):
    inv = 1.0 / (ROPE_THETA ** (jnp.arange(0, HEAD_DIM, 2, dtype=jnp.float32) / HEAD_DIM))
    ang = jnp.arange(length, dtype=jnp.float32)[:, None] * inv[None, :]
    return jnp.cos(ang), jnp.sin(ang)


def apply_rope(x, cos, sin):
    c = cos[None, :, None, None, :].astype(x.dtype)
    s = sin[None, :, None, None, :].astype(x.dtype)
    half = HEAD_DIM // 2
    x1, x2 = x[..., :half], x[..., half:]
    return jnp.concatenate([x1 * c - x2 * s, x2 * c + x1 * s], axis=-1)


def causal_depthwise_conv(x, w, b):
    y = lax.conv_general_dilated(
        x, w[:, None, :].astype(x.dtype), window_strides=(1,),
        padding=[(CONV_WIDTH - 1, 0)], dimension_numbers=('NWC', 'WIO', 'NWC'),
        feature_group_count=x.shape[-1])
    return y + b.astype(x.dtype)


def _linear_combine(left, right):
    a_l, b_l = left
    a_r, b_r = right
    return a_l * a_r, a_r * b_l + b_r


def rglru_block(u, w_in, conv_w, conv_b, w_r, b_r, w_i, b_i, lam, w_out):
    B, L, _ = u.shape
    proj = u @ w_in
    gate = jax.nn.gelu(proj[..., :D_RNN])
    xr = causal_depthwise_conv(proj[..., D_RNN:], conv_w, conv_b)
    xb = xr.reshape(B, L, N_GATE_BLOCKS, GATE_BLOCK)
    r = jax.nn.sigmoid(jnp.einsum('blnc,ncd->blnd', xb, w_r).reshape(B, L, D_RNN).astype(jnp.float32)
                       + b_r.astype(jnp.float32))
    i = jax.nn.sigmoid(jnp.einsum('blnc,ncd->blnd', xb, w_i).reshape(B, L, D_RNN).astype(jnp.float32)
                       + b_i.astype(jnp.float32))
    log_a = -LRU_C * r * jax.nn.softplus(-lam.astype(jnp.float32))
    a = jnp.exp(log_a)
    bvals = jnp.sqrt(-jnp.expm1(2.0 * log_a)) * (i * xr.astype(jnp.float32))
    _, hs = lax.associative_scan(_linear_combine, (a, bvals), axis=1)
    y = hs.astype(u.dtype) * gate
    return y @ w_out


def squared_relu_mlp(u, w1, w2):
    return jnp.square(jax.nn.relu(u @ w1)) @ w2


def shared_kv(h, g, w_kv, cos, sin):
    B, L, _ = h.shape
    kv = rmsnorm(h, g) @ w_kv
    k = apply_rope(kv[..., :QK_WIDTH].reshape(B, L, N_DIFF_HEADS, 2, HEAD_DIM), cos, sin)
    v = kv[..., QK_WIDTH:].reshape(B, L, N_DIFF_HEADS, 2 * HEAD_DIM)
    return k, v


def causal_diff_attention(q, k, v, lam):
    L = q.shape[1]
    scale = 1.0 / math.sqrt(HEAD_DIM)
    bounds = [0] + list(range(N_META, L, Q_BLOCK)) + [L]
    outs = []
    for s, e in zip(bounds[:-1], bounds[1:]):
        scores = jnp.einsum('bqhcd,bkhcd->bhcqk', q[:, s:e], k[:, :e],
                            preferred_element_type=jnp.float32) * scale
        mask = jnp.arange(e)[None, :] <= (s + jnp.arange(e - s))[:, None]
        p = jax.nn.softmax(jnp.where(mask, scores, -jnp.inf), axis=-1)
        attn = p[:, :, 0] - lam * p[:, :, 1]
        outs.append(jnp.einsum('bhqk,bkhe->bqhe', attn.astype(v.dtype), v[:, :e]))
    return jnp.concatenate(outs, axis=1)


def diff_attention_layer(u, k, v, w_q, lam_vecs, subln_g, w_o, cos, sin, lambda_init):
    B, L, _ = u.shape
    q = apply_rope((u @ w_q).reshape(B, L, N_DIFF_HEADS, 2, HEAD_DIM), cos, sin)
    lv = lam_vecs.astype(jnp.float32)
    lam = jnp.exp(jnp.sum(lv[0] * lv[1])) - jnp.exp(jnp.sum(lv[2] * lv[3])) + lambda_init
    o = causal_diff_attention(q, k, v, lam)
    o = rmsnorm(o, subln_g, SUBLN_EPS) * (1.0 - lambda_init)
    return o.reshape(B, L, V_WIDTH) @ w_o


def setup_inputs(seed: int = 0) -> dict:
    key = jax.random.key(seed)
    ks = jax.random.split(key, 24)
    f32 = jnp.float32

    def nrm(k, shape, fan_in):
        return jax.random.normal(k, shape, f32) * (fan_in ** -0.5)

    def gain(k, shape):
        return 1.0 + 0.05 * jax.random.normal(k, shape, f32)

    u = jax.random.uniform(ks[10], (N_A_LAYERS, D_RNN), f32, minval=0.9, maxval=0.999)
    a0 = u ** (1.0 / LRU_C)
    a_lambda = jnp.log(a0) - jnp.log1p(-a0)
    return {
        "x": jax.random.normal(ks[0], (BATCH, SEQ, D_MODEL), f32),
        "meta_tokens": jax.random.normal(ks[1], (N_META, D_MODEL), f32),
        "a_norm_g": gain(ks[2], (N_A_LAYERS, D_MODEL)),
        "a_w_in": nrm(ks[3], (N_A_LAYERS, D_MODEL, 2 * D_RNN), D_MODEL),
        "a_conv_w": nrm(ks[4], (N_A_LAYERS, CONV_WIDTH, D_RNN), CONV_WIDTH),
        "a_conv_b": 0.02 * jax.random.normal(ks[5], (N_A_LAYERS, D_RNN), f32),
        "a_w_r": nrm(ks[6], (N_A_LAYERS, N_GATE_BLOCKS, GATE_BLOCK, GATE_BLOCK), GATE_BLOCK),
        "a_b_r": 0.02 * jax.random.normal(ks[7], (N_A_LAYERS, D_RNN), f32),
        "a_w_i": nrm(ks[8], (N_A_LAYERS, N_GATE_BLOCKS, GATE_BLOCK, GATE_BLOCK), GATE_BLOCK),
        "a_b_i": 0.02 * jax.random.normal(ks[9], (N_A_LAYERS, D_RNN), f32),
        "a_lambda": a_lambda,
        "a_w_out": nrm(ks[11], (N_A_LAYERS, D_RNN, D_MODEL), D_RNN),
        "kv_norm_g": gain(ks[12], (D_MODEL,)),
        "w_kv": nrm(ks[13], (D_MODEL, QK_WIDTH + V_WIDTH), D_MODEL),
        "b_norm_g": gain(ks[14], (N_B_LAYERS, D_MODEL)),
        "b_w_q": nrm(ks[15], (N_B_LAYERS, D_MODEL, QK_WIDTH), D_MODEL),
        "b_lambda": 0.1 * jax.random.normal(ks[16], (N_B_LAYERS, 4, HEAD_DIM), f32),
        "b_subln_g": gain(ks[17], (N_B_LAYERS, 2 * HEAD_DIM)),
        "b_w_o": nrm(ks[18], (N_B_LAYERS, V_WIDTH, D_MODEL), V_WIDTH),
        "mlp_norm_g": gain(ks[19], (DEPTH, D_MODEL)),
        "mlp_w1": nrm(ks[20], (DEPTH, D_MODEL, D_FF), D_MODEL),
        "mlp_w2": nrm(ks[21], (DEPTH, D_FF, D_MODEL), D_FF),
        "final_norm_g": gain(ks[22], (D_MODEL,)),
    }


def reference(x, meta_tokens, a_norm_g, a_w_in, a_conv_w, a_conv_b, a_w_r, a_b_r, a_w_i, a_b_i,
              a_lambda, a_w_out, kv_norm_g, w_kv, b_norm_g, b_w_q, b_lambda, b_subln_g, b_w_o,
              mlp_norm_g, mlp_w1, mlp_w2, final_norm_g):
    B = x.shape[0]
    meta = jnp.broadcast_to(meta_tokens[None].astype(x.dtype), (B, N_META, D_MODEL))
    h = jnp.concatenate([meta, x], axis=1)
    cos, sin = rope_tables(h.shape[1])
    k_sh, v_sh = None, None
    for layer in range(DEPTH):
        if layer < N_A_LAYERS:
            j = layer
            h = h + rglru_block(rmsnorm(h, a_norm_g[j]), a_w_in[j], a_conv_w[j], a_conv_b[j],
                                a_w_r[j], a_b_r[j], a_w_i[j], a_b_i[j], a_lambda[j], a_w_out[j])
        else:
            j = layer - N_A_LAYERS
            if j == 0:
                k_sh, v_sh = shared_kv(h, kv_norm_g, w_kv, cos, sin)
            lambda_init = 0.8 - 0.6 * math.exp(-0.3 * layer)
            h = h + diff_attention_layer(rmsnorm(h, b_norm_g[j]), k_sh, v_sh, b_w_q[j], b_lambda[j],
                                         b_subln_g[j], b_w_o[j], cos, sin, lambda_init)
        h = h + squared_relu_mlp(rmsnorm(h, mlp_norm_g[layer]), mlp_w1[layer], mlp_w2[layer])
    h = rmsnorm(h, final_norm_g)
    return h[:, N_META:]
```

```python
import functools
import math

import jax
import jax.numpy as jnp
from jax import lax
from jax.experimental import pallas as pl
from jax.experimental.pallas import tpu as pltpu

F32 = jnp.float32
BF16 = jnp.bfloat16

N_META_TOKENS = 16
N_GATE_BLOCKS = 16
CONV_WIDTH = 4
LRU_C = 8.0
HEAD_DIM = 128
ROPE_THETA = 10000.0
NORM_EPS = 1e-6
SUBLN_EPS = 1e-5

LANES = 128
SUBLANES = 8
VMEM_LIMIT_BYTES = 56 * 1024 * 1024

SEQ_TILE = 384
TOKEN_TILE = 768
COL_TILE = 512
FF_TILE = 512
GATE_GROUP = 4


def _params(*semantics):
    return pltpu.CompilerParams(dimension_semantics=semantics,
                                vmem_limit_bytes=VMEM_LIMIT_BYTES)


def _rms_scale(x, eps):
    return lax.rsqrt(jnp.mean(x * x, axis=-1, keepdims=True) + eps)


def _norm_matmul_kernel(*refs, act, rope, scale):
    if rope:
        x_ref, g_ref, w_ref, cos_ref, sin_ref, o_ref, xn_ref = refs
    else:
        x_ref, g_ref, w_ref, o_ref, xn_ref = refs

    @pl.when(pl.program_id(1) == 0)
    def _():
        x = x_ref[...]
        xn_ref[...] = (x * _rms_scale(x, NORM_EPS) * g_ref[...]).astype(xn_ref.dtype)

    y = jnp.dot(xn_ref[...], w_ref[...], preferred_element_type=F32)
    if act == "gelu":
        y = jax.nn.gelu(y)
    if rope:
        c = cos_ref[...]
        s = sin_ref[...]
        segs = []
        for i in range(y.shape[1] // HEAD_DIM):
            seg = y[:, i * HEAD_DIM:(i + 1) * HEAD_DIM]
            segs.append(seg * c + pltpu.roll(seg, HEAD_DIM // 2, axis=1) * s)
        y = jnp.concatenate(segs, axis=1)
    if scale != 1.0:
        y = y * scale
    o_ref[...] = y.astype(o_ref.dtype)


def _norm_matmul(x, g, w, out_dtype, act=None, rope=None, scale=1.0):
    t, d = x.shape
    n = w.shape[1]
    tm, tn = TOKEN_TILE, COL_TILE
    in_specs = [
        pl.BlockSpec((tm, d), lambda i, j: (i, 0)),
        pl.BlockSpec((1, d), lambda i, j: (0, 0)),
        pl.BlockSpec((d, tn), lambda i, j: (0, j)),
    ]
    args = [x, g.reshape(1, d), w]
    if rope is not None:
        in_specs += [pl.BlockSpec((tm, HEAD_DIM), lambda i, j: (i, 0))] * 2
        args += list(rope)
    return pl.pallas_call(
        functools.partial(_norm_matmul_kernel, act=act, rope=rope is not None, scale=scale),
        grid=(t // tm, n // tn),
        in_specs=in_specs,
        out_specs=pl.BlockSpec((tm, tn), lambda i, j: (i, j)),
        out_shape=jax.ShapeDtypeStruct((t, n), out_dtype),
        scratch_shapes=[pltpu.VMEM((tm, d), BF16)],
        compiler_params=_params("parallel", "arbitrary"),
    )(*args)


def _matmul_residual_kernel(y_ref, w_ref, r_ref, o_ref):
    o_ref[...] = r_ref[...] + jnp.dot(y_ref[...], w_ref[...], preferred_element_type=F32)


def _matmul_residual(y, w, res):
    t, k = y.shape
    n = w.shape[1]
    tm, tn = TOKEN_TILE, COL_TILE
    return pl.pallas_call(
        _matmul_residual_kernel,
        grid=(t // tm, n // tn),
        in_specs=[
            pl.BlockSpec((tm, k), lambda i, j: (i, 0)),
            pl.BlockSpec((k, tn), lambda i, j: (0, j)),
            pl.BlockSpec((tm, tn), lambda i, j: (i, j)),
        ],
        out_specs=pl.BlockSpec((tm, tn), lambda i, j: (i, j)),
        out_shape=jax.ShapeDtypeStruct((t, n), F32),
        compiler_params=_params("parallel", "arbitrary"),
    )(y, w, res)


def _mlp_kernel(*refs, final):
    if final:
        x_ref, g_ref, w1_ref, w2_ref, fg_ref, o_ref, xn_ref = refs
    else:
        x_ref, g_ref, w1_ref, w2_ref, o_ref, xn_ref = refs
    j = pl.program_id(1)

    @pl.when(j == 0)
    def _():
        x = x_ref[...]
        xn_ref[...] = (x * _rms_scale(x, NORM_EPS) * g_ref[...]).astype(xn_ref.dtype)
        o_ref[...] = x

    h = jnp.dot(xn_ref[...], w1_ref[...], preferred_element_type=F32)
    h = jnp.square(jnp.maximum(h, 0.0)).astype(BF16)
    o_ref[...] += jnp.dot(h, w2_ref[...], preferred_element_type=F32)

    if final:
        @pl.when(j == pl.num_programs(1) - 1)
        def _():
            y = o_ref[...]
            o_ref[...] = y * _rms_scale(y, NORM_EPS) * fg_ref[...]


def _mlp(x, g, w1, w2, final_g=None):
    t, d = x.shape
    f = w1.shape[1]
    tm, tf = TOKEN_TILE, FF_TILE
    in_specs = [
        pl.BlockSpec((tm, d), lambda i, j: (i, 0)),
        pl.BlockSpec((1, d), lambda i, j: (0, 0)),
        pl.BlockSpec((d, tf), lambda i, j: (0, j)),
        pl.BlockSpec((tf, d), lambda i, j: (j, 0)),
    ]
    args = [x, g.reshape(1, d), w1, w2]
    if final_g is not None:
        in_specs.append(pl.BlockSpec((1, d), lambda i, j: (0, 0)))
        args.append(final_g.reshape(1, d))
    return pl.pallas_call(
        functools.partial(_mlp_kernel, final=final_g is not None),
        grid=(t // tm, f // tf),
        in_specs=in_specs,
        out_specs=pl.BlockSpec((tm, d), lambda i, j: (i, 0)),
        out_shape=jax.ShapeDtypeStruct((t, d), F32),
        scratch_shapes=[pltpu.VMEM((tm, d), BF16)],
        compiler_params=_params("parallel", "arbitrary"),
    )(*args)


def _rglru_kernel(xr_ref, gate_ref, cw_ref, cb_ref, w_ref, b_ref, lam_ref, y_ref,
                  xbuf, a_scr, b_scr, h_scr, aend, bend, cin, hcar, *, ts):
    c = xr_ref.shape[2]
    seg = ts // SUBLANES

    @pl.when(pl.program_id(2) == 0)
    def _():
        xbuf[0:SUBLANES, :] = jnp.zeros((SUBLANES, c), F32)
        hcar[...] = jnp.zeros(hcar.shape, F32)

    x = xr_ref[0]
    xbuf[SUBLANES:SUBLANES + ts, :] = x
    cw = cw_ref[...]
    xc = x * cw[CONV_WIDTH - 1:CONV_WIDTH]
    for tap in range(1, CONV_WIDTH):
        row = CONV_WIDTH - 1 - tap
        xc = xc + xbuf[SUBLANES - tap:SUBLANES - tap + ts, :] * cw[row:row + 1]
    xc = xc + cb_ref[...]
    xbuf[0:SUBLANES, :] = xbuf[ts:ts + SUBLANES, :]

    gates = jnp.dot(xc.astype(BF16), w_ref[0], preferred_element_type=F32) + b_ref[0]
    r = jax.nn.sigmoid(gates[:, :c])
    i = jax.nn.sigmoid(gates[:, c:])
    log_a = r * (-LRU_C * jax.nn.softplus(-lam_ref[...]))
    a = jnp.exp(log_a)
    bv = jnp.sqrt(-jnp.tanh(log_a) * (a * a + 1.0)) * (i * xc)
    n_lane_tiles = c // LANES
    for j in range(n_lane_tiles):
        a_scr[j] = a[:, j * LANES:(j + 1) * LANES]
        b_scr[j] = bv[:, j * LANES:(j + 1) * LANES]

    def rows(k):
        return pl.ds(k, SUBLANES, stride=seg)

    a_cum = a_scr[:, rows(0), :]
    b_cum = b_scr[:, rows(0), :]
    for k in range(1, seg):
        a_k = a_scr[:, rows(k), :]
        b_cum = a_k * b_cum + b_scr[:, rows(k), :]
        a_cum = a_k * a_cum
        a_scr[:, rows(k), :] = a_cum
        b_scr[:, rows(k), :] = b_cum
    aend[...] = a_cum
    bend[...] = b_cum

    carry = hcar[:, 0:1, :]
    for s in range(SUBLANES):
        cin[:, s:s + 1, :] = carry
        carry = aend[:, s:s + 1, :] * carry + bend[:, s:s + 1, :]
    hcar[:, 0:1, :] = carry
    h_in = cin[...]

    for k in range(seg):
        h_scr[:, rows(k), :] = a_scr[:, rows(k), :] * h_in + b_scr[:, rows(k), :]
    for j in range(n_lane_tiles):
        lanes = slice(j * LANES, (j + 1) * LANES)
        y_ref[0, :, lanes] = (h_scr[j] * gate_ref[0, :, lanes].astype(F32)).astype(y_ref.dtype)


def _rglru(xr, gate, conv_w, conv_b, w_gates, b_gates, lam):
    b, l, c = xr.shape
    groups = w_gates.shape[0]
    cg = c // groups
    ts = SEQ_TILE
    seq_spec = pl.BlockSpec((1, ts, cg), lambda bi, gi, ci: (bi, ci, gi))
    chan_spec = lambda rows: pl.BlockSpec((rows, cg), lambda bi, gi, ci: (0, gi))
    return pl.pallas_call(
        functools.partial(_rglru_kernel, ts=ts),
        grid=(b, groups, l // ts),
        in_specs=[
            seq_spec, seq_spec,
            chan_spec(CONV_WIDTH), chan_spec(1),
            pl.BlockSpec((1, cg, 2 * cg), lambda bi, gi, ci: (gi, 0, 0)),
            pl.BlockSpec((1, 1, 2 * cg), lambda bi, gi, ci: (gi, 0, 0)),
            chan_spec(1),
        ],
        out_specs=seq_spec,
        out_shape=jax.ShapeDtypeStruct((b, l, c), BF16),
        scratch_shapes=[
            pltpu.VMEM((ts + SUBLANES, cg), F32),
            pltpu.VMEM((cg // LANES, ts, LANES), F32),
            pltpu.VMEM((cg // LANES, ts, LANES), F32),
            pltpu.VMEM((cg // LANES, ts, LANES), F32),
            pltpu.VMEM((cg // LANES, SUBLANES, LANES), F32),
            pltpu.VMEM((cg // LANES, SUBLANES, LANES), F32),
            pltpu.VMEM((cg // LANES, SUBLANES, LANES), F32),
            pltpu.VMEM((cg // LANES, SUBLANES, LANES), F32),
        ],
        compiler_params=_params("parallel", "parallel", "arbitrary"),
    )(xr, gate, conv_w, conv_b.reshape(1, c), w_gates, b_gates, lam.reshape(1, c))


def _gate_weights(w_r, b_r, w_i, b_i):
    nb, gb, _ = w_r.shape
    groups = nb // GATE_GROUP
    eye = jnp.eye(GATE_GROUP, dtype=w_r.dtype)

    def dense(w):
        w = w.reshape(groups, GATE_GROUP, gb, gb)
        w = jnp.einsum("gjcd,jk->gjckd", w, eye)
        return w.reshape(groups, GATE_GROUP * gb, GATE_GROUP * gb)

    w = jnp.concatenate([dense(w_r), dense(w_i)], axis=2).astype(BF16)
    cg = GATE_GROUP * gb
    b = jnp.concatenate([b_r.reshape(groups, 1, cg), b_i.reshape(groups, 1, cg)], axis=2)
    return w, b.astype(F32)


def _attn_kernel(lam_ref, g_ref, q_ref, k_ref, v_ref, o_ref, acc0, acc1, *, tq, lambda_init):
    qi = pl.program_id(2)
    q = q_ref[0]
    q_parts = (q[:, :HEAD_DIM], q[:, HEAD_DIM:])
    accs = (acc0, acc1)
    acc0[...] = jnp.zeros(acc0.shape, F32)
    acc1[...] = jnp.zeros(acc1.shape, F32)

    def step(start, carry, masked):
        k = k_ref[0, pl.ds(start, tq), :]
        v = v_ref[0, pl.ds(start, tq), :]
        out = []
        for comp in range(2):
            m_prev, l_prev = carry[2 * comp], carry[2 * comp + 1]
            s = lax.dot_general(q_parts[comp], k[:, comp * HEAD_DIM:(comp + 1) * HEAD_DIM],
                                (((1,), (1,)), ((), ())), preferred_element_type=F32)
            if masked:
                row = lax.broadcasted_iota(jnp.int32, s.shape, 0)
                col = lax.broadcasted_iota(jnp.int32, s.shape, 1)
                s = jnp.where(col <= row, s, -jnp.inf)
            m_new = jnp.maximum(m_prev, jnp.max(s, axis=-1, keepdims=True))
            p = jnp.exp(s - m_new)
            alpha = jnp.exp(m_prev - m_new)
            l_new = alpha * l_prev + jnp.sum(p, axis=-1, keepdims=True)
            accs[comp][...] = alpha * accs[comp][...] + jnp.dot(
                p.astype(BF16), v, preferred_element_type=F32)
            out += [m_new, l_new]
        return tuple(out)

    neg = jnp.full((tq, 1), -jnp.inf, F32)
    zero = jnp.zeros((tq, 1), F32)
    carry = lax.fori_loop(
        0, qi, lambda ki, cr: step(pl.multiple_of(ki * tq, tq), cr, False),
        (neg, zero, neg, zero))
    _, l0, _, l1 = step(pl.multiple_of(qi * tq, tq), carry, True)

    lv = lam_ref[0]
    lam = (jnp.exp(jnp.sum(lv[0:1] * lv[1:2], axis=-1, keepdims=True))
           - jnp.exp(jnp.sum(lv[2:3] * lv[3:4], axis=-1, keepdims=True)) + lambda_init)
    o = acc0[...] / l0 - lam * (acc1[...] / l1)
    o = o * _rms_scale(o, SUBLN_EPS) * g_ref[...] * (1.0 - lambda_init)
    o_ref[0] = o.astype(o_ref.dtype)


def _diff_attention(q, k, v, lam_vecs, subln_g, lambda_init):
    b, l, w = q.shape
    hw = 2 * HEAD_DIM
    tq = SEQ_TILE
    kv_spec = pl.BlockSpec((1, l, hw), lambda bi, hi, qi: (bi, 0, hi))
    q_spec = pl.BlockSpec((1, tq, hw), lambda bi, hi, qi: (bi, qi, hi))
    return pl.pallas_call(
        functools.partial(_attn_kernel, tq=tq, lambda_init=lambda_init),
        grid=(b, w // hw, l // tq),
        in_specs=[
            pl.BlockSpec((1, 4, HEAD_DIM), lambda bi, hi, qi: (0, 0, 0)),
            pl.BlockSpec((1, hw), lambda bi, hi, qi: (0, 0)),
            q_spec, kv_spec, kv_spec,
        ],
        out_specs=q_spec,
        out_shape=jax.ShapeDtypeStruct((b, l, w), BF16),
        scratch_shapes=[pltpu.VMEM((tq, hw), F32), pltpu.VMEM((tq, hw), F32)],
        compiler_params=_params("parallel", "parallel", "arbitrary"),
    )(lam_vecs.reshape(1, 4, HEAD_DIM), subln_g.reshape(1, hw), q, k, v)


def _rope_tables(length, batch):
    inv = 1.0 / (ROPE_THETA ** (jnp.arange(0, HEAD_DIM, 2, dtype=F32) / HEAD_DIM))
    ang = jnp.arange(length, dtype=F32)[:, None] * inv[None, :]
    cos, sin = jnp.cos(ang), jnp.sin(ang)
    cos2 = jnp.concatenate([cos, cos], axis=1)
    sin2 = jnp.concatenate([-sin, sin], axis=1)
    return jnp.tile(cos2, (batch, 1)), jnp.tile(sin2, (batch, 1))


def kernel(x, meta_tokens, a_norm_g, a_w_in, a_conv_w, a_conv_b, a_w_r, a_b_r, a_w_i, a_b_i,
           a_lambda, a_w_out, kv_norm_g, w_kv, b_norm_g, b_w_q, b_lambda, b_subln_g, b_w_o,
           mlp_norm_g, mlp_w1, mlp_w2, final_norm_g):
    batch, seq, d = x.shape
    n_a = a_w_in.shape[0]
    depth = mlp_w1.shape[0]
    d_rnn = a_w_out.shape[1]
    qk_width = b_w_q.shape[2]
    length = N_META_TOKENS + seq
    lp = pl.cdiv(length, SEQ_TILE) * SEQ_TILE
    t = batch * lp
    assert t % TOKEN_TILE == 0

    meta = jnp.broadcast_to(meta_tokens[None].astype(x.dtype), (batch, N_META_TOKENS, d))
    pad = jnp.zeros((batch, lp - length, d), x.dtype)
    h = jnp.concatenate([meta, x, pad], axis=1).reshape(t, d)
    rope = _rope_tables(lp, batch)

    k_sh = v_sh = None
    for layer in range(depth):
        if layer < n_a:
            j = layer
            w_in = a_w_in[j].astype(BF16)
            gate = _norm_matmul(h, a_norm_g[j], w_in[:, :d_rnn], BF16, act="gelu")
            xr = _norm_matmul(h, a_norm_g[j], w_in[:, d_rnn:], F32)
            w_g, b_g = _gate_weights(a_w_r[j], a_b_r[j], a_w_i[j], a_b_i[j])
            y = _rglru(xr.reshape(batch, lp, d_rnn), gate.reshape(batch, lp, d_rnn),
                       a_conv_w[j], a_conv_b[j], w_g, b_g, a_lambda[j])
            h = _matmul_residual(y.reshape(t, d_rnn), a_w_out[j].astype(BF16), h)
        else:
            j = layer - n_a
            if j == 0:
                wkv = w_kv.astype(BF16)
                k_sh = _norm_matmul(h, kv_norm_g, wkv[:, :qk_width], BF16, rope=rope)
                v_sh = _norm_matmul(h, kv_norm_g, wkv[:, qk_width:], BF16)
                k_sh = k_sh.reshape(batch, lp, -1)
                v_sh = v_sh.reshape(batch, lp, -1)
            lambda_init = 0.8 - 0.6 * math.exp(-0.3 * layer)
            q = _norm_matmul(h, b_norm_g[j], b_w_q[j].astype(BF16), BF16, rope=rope,
                             scale=1.0 / math.sqrt(HEAD_DIM))
            o = _diff_attention(q.reshape(batch, lp, -1), k_sh, v_sh, b_lambda[j],
                                b_subln_g[j], lambda_init)
            h = _matmul_residual(o.reshape(t, -1), b_w_o[j].astype(BF16), h)
        final_g = final_norm_g if layer == depth - 1 else None
        h = _mlp(h, mlp_norm_g[layer], mlp_w1[layer].astype(BF16), mlp_w2[layer].astype(BF16),
                 final_g)
    return h.reshape(batch, lp, d)[:, N_META_TOKENS:length]
```

```python
import functools
import math

import jax
import jax.numpy as jnp
from jax import lax
from jax.experimental import pallas as pl
from jax.experimental.pallas import tpu as pltpu

F32 = jnp.float32
BF16 = jnp.bfloat16

N_META_TOKENS = 16
CONV_WIDTH = 4
LRU_C = 8.0
HEAD_DIM = 128
ROPE_THETA = 10000.0
NORM_EPS = 1e-6
SUBLN_EPS = 1e-5

LANES = 128
SUBLANES = 8
VMEM_LIMIT_BYTES = 56 * 1024 * 1024

SEQ_TILE = 384
TOKEN_TILE = 768
COL_TILES = (1280, 1024, 512)
FF_TILE = 512
GATE_GROUP = 4
SCAN_ROWS = 264
RGLRU_TILE = 528
KV_BLOCKS_PER_STEP = 2
HEADS_PER_STEP = 2


def _params(*semantics):
    return pltpu.CompilerParams(dimension_semantics=semantics,
                                vmem_limit_bytes=VMEM_LIMIT_BYTES)


def _col_tile(n):
    return next(t for t in COL_TILES if n % t == 0)


def _rms_scale(x, eps):
    return lax.rsqrt(jnp.mean(x * x, axis=-1, keepdims=True) + eps)


def _norm_matmul_kernel(*refs, act, rope, scale):
    if rope:
        x_ref, g_ref, w_ref, cos_ref, sin_ref, o_ref, xn_ref = refs
    else:
        x_ref, g_ref, w_ref, o_ref, xn_ref = refs

    @pl.when(pl.program_id(1) == 0)
    def _():
        x = x_ref[...]
        xn_ref[...] = (x * _rms_scale(x, NORM_EPS) * g_ref[...]).astype(xn_ref.dtype)

    y = jnp.dot(xn_ref[...], w_ref[...], preferred_element_type=F32)
    if act == "gelu":
        y = jax.nn.gelu(y)
    if rope:
        c = cos_ref[...]
        s = sin_ref[...]
        segs = []
        for i in range(y.shape[1] // HEAD_DIM):
            seg = y[:, i * HEAD_DIM:(i + 1) * HEAD_DIM]
            segs.append(seg * c + pltpu.roll(seg, HEAD_DIM // 2, axis=1) * s)
        y = jnp.concatenate(segs, axis=1)
    if scale != 1.0:
        y = y * scale
    o_ref[...] = y.astype(o_ref.dtype)


def _norm_matmul(x, g, w, col0, n, out_dtype, act=None, rope=None, scale=1.0):
    t, d = x.shape
    tm, tn = TOKEN_TILE, _col_tile(n)
    assert col0 % tn == 0
    j0 = col0 // tn
    in_specs = [
        pl.BlockSpec((tm, d), lambda i, j: (i, 0)),
        pl.BlockSpec((1, d), lambda i, j: (0, 0)),
        pl.BlockSpec((d, tn), lambda i, j: (0, j0 + j)),
    ]
    args = [x, g.reshape(1, d), w]
    if rope is not None:
        in_specs += [pl.BlockSpec((tm, HEAD_DIM), lambda i, j: (i, 0))] * 2
        args += list(rope)
    return pl.pallas_call(
        functools.partial(_norm_matmul_kernel, act=act, rope=rope is not None, scale=scale),
        grid=(t // tm, n // tn),
        in_specs=in_specs,
        out_specs=pl.BlockSpec((tm, tn), lambda i, j: (i, j)),
        out_shape=jax.ShapeDtypeStruct((t, n), out_dtype),
        scratch_shapes=[pltpu.VMEM((tm, d), BF16)],
        compiler_params=_params("parallel", "arbitrary"),
    )(*args)


def _matmul_residual_kernel(y_ref, w_ref, r_ref, o_ref):
    o_ref[...] = r_ref[...] + jnp.dot(y_ref[...], w_ref[...], preferred_element_type=F32)


def _matmul_residual(y, w, res):
    t, k = y.shape
    n = w.shape[1]
    tm, tn = TOKEN_TILE, _col_tile(n)
    return pl.pallas_call(
        _matmul_residual_kernel,
        grid=(t // tm, n // tn),
        in_specs=[
            pl.BlockSpec((tm, k), lambda i, j: (i, 0)),
            pl.BlockSpec((k, tn), lambda i, j: (0, j)),
            pl.BlockSpec((tm, tn), lambda i, j: (i, j)),
        ],
        out_specs=pl.BlockSpec((tm, tn), lambda i, j: (i, j)),
        out_shape=jax.ShapeDtypeStruct((t, n), F32),
        compiler_params=_params("parallel", "arbitrary"),
    )(y, w, res)


def _mlp_kernel(*refs, final):
    if final:
        x_ref, g_ref, w1_ref, w2_ref, fg_ref, o_ref, xn_ref = refs
    else:
        x_ref, g_ref, w1_ref, w2_ref, o_ref, xn_ref = refs
    j = pl.program_id(1)

    @pl.when(j == 0)
    def _():
        x = x_ref[...]
        xn_ref[...] = (x * _rms_scale(x, NORM_EPS) * g_ref[...]).astype(xn_ref.dtype)
        o_ref[...] = x

    h = jnp.dot(xn_ref[...], w1_ref[...], preferred_element_type=F32)
    h = jnp.square(jnp.maximum(h, 0.0)).astype(BF16)
    o_ref[...] += jnp.dot(h, w2_ref[...], preferred_element_type=F32)

    if final:
        @pl.when(j == pl.num_programs(1) - 1)
        def _():
            y = o_ref[...]
            o_ref[...] = y * _rms_scale(y, NORM_EPS) * fg_ref[...]


def _mlp(x, g, w1, w2, final_g=None):
    t, d = x.shape
    f = w1.shape[1]
    tm, tf = TOKEN_TILE, FF_TILE
    in_specs = [
        pl.BlockSpec((tm, d), lambda i, j: (i, 0)),
        pl.BlockSpec((1, d), lambda i, j: (0, 0)),
        pl.BlockSpec((d, tf), lambda i, j: (0, j)),
        pl.BlockSpec((tf, d), lambda i, j: (j, 0)),
    ]
    args = [x, g.reshape(1, d), w1, w2]
    if final_g is not None:
        in_specs.append(pl.BlockSpec((1, d), lambda i, j: (0, 0)))
        args.append(final_g.reshape(1, d))
    return pl.pallas_call(
        functools.partial(_mlp_kernel, final=final_g is not None),
        grid=(t // tm, f // tf),
        in_specs=in_specs,
        out_specs=pl.BlockSpec((tm, d), lambda i, j: (i, 0)),
        out_shape=jax.ShapeDtypeStruct((t, d), F32),
        scratch_shapes=[pltpu.VMEM((tm, d), BF16)],
        compiler_params=_params("parallel", "arbitrary"),
    )(*args)


def _rglru_kernel(xr_ref, gate_ref, cw_ref, cb_ref, w_ref, b_ref, lam_ref, y_ref,
                  xl, xc_scr, hl, aend, bend, cin, hcar, *, ts):
    c = xr_ref.shape[2]
    n_lane_tiles = c // LANES
    lane_tiles = range(n_lane_tiles)
    seg = SCAN_ROWS // SUBLANES
    n_groups = ts // SCAN_ROWS
    hdr = SUBLANES

    @pl.when(pl.program_id(2) == 0)
    def _():
        xl[:, 0:hdr, :] = jnp.zeros((n_lane_tiles, hdr, LANES), F32)
        hcar[...] = jnp.zeros(hcar.shape, F32)

    for j in lane_tiles:
        xl[j, hdr:hdr + ts, :] = xr_ref[0, :, j * LANES:(j + 1) * LANES]

    def lane(x, j):
        return x[:, j * LANES:(j + 1) * LANES]

    taps = [cw_ref[tap] for tap in range(CONV_WIDTH)]
    bias = cb_ref[...]
    for g in range(n_groups):
        loads = {}
        for k in range(-(CONV_WIDTH - 1), seg):
            loads[k] = xl[:, pl.ds(hdr + g * SCAN_ROWS + k, SUBLANES, stride=seg), :]
        for k in range(seg):
            xc = loads[k] * taps[CONV_WIDTH - 1] + bias
            for back in range(1, CONV_WIDTH):
                xc = xc + loads[k - back] * taps[CONV_WIDTH - 1 - back]
            r0 = g * SCAN_ROWS + k * SUBLANES
            for j in lane_tiles:
                xc_scr[r0:r0 + SUBLANES, j * LANES:(j + 1) * LANES] = xc[j]
    xl[:, 0:hdr, :] = xl[:, ts:ts + hdr, :]

    xc = xc_scr[...]
    half_gates = jnp.dot(xc.astype(BF16), w_ref[0], preferred_element_type=F32) + b_ref[0]
    t_r = jnp.tanh(half_gates[:, :c])
    t_i = jnp.tanh(half_gates[:, c:])
    half_rate = (-0.5 * LRU_C * math.log2(math.e)) * jax.nn.softplus(-lam_ref[...])
    a = jnp.exp2(t_r * half_rate + half_rate)
    bv = jnp.exp(0.5 * jnp.log(1.0 - a * a)) * ((0.5 * t_i + 0.5) * xc)

    def rows(x, g, k):
        r0 = g * SCAN_ROWS + k * SUBLANES
        return x[r0:r0 + SUBLANES, :]

    for g in range(n_groups):
        a_cum, b_cum = rows(a, g, 0), rows(bv, g, 0)
        for k in range(1, seg):
            a_k = rows(a, g, k)
            b_cum = a_k * b_cum + rows(bv, g, k)
            a_cum = a_k * a_cum
        aend[g * SUBLANES:(g + 1) * SUBLANES, :] = a_cum
        bend[g * SUBLANES:(g + 1) * SUBLANES, :] = b_cum

    carry = hcar[0:1, :]
    for s in range(n_groups * SUBLANES):
        cin[s:s + 1, :] = carry
        carry = aend[s:s + 1, :] * carry + bend[s:s + 1, :]
    hcar[0:1, :] = carry

    for g in range(n_groups):
        h = cin[g * SUBLANES:(g + 1) * SUBLANES, :]
        for k in range(seg):
            h = rows(a, g, k) * h + rows(bv, g, k)
            for j in lane_tiles:
                hl[j, pl.ds(g * SCAN_ROWS + k, SUBLANES, stride=seg), :] = lane(h, j)
    for j in lane_tiles:
        lanes = slice(j * LANES, (j + 1) * LANES)
        y_ref[0, :, lanes] = (hl[j] * gate_ref[0, :, lanes].astype(F32)).astype(y_ref.dtype)


def _rglru(xr, gate, conv_w, conv_b, w_gates, b_gates, lam):
    b, l, c = xr.shape
    groups = w_gates.shape[0]
    cg = c // groups
    nl = cg // LANES
    ts = RGLRU_TILE
    assert l % ts == 0
    n_runs = ts // SCAN_ROWS * SUBLANES
    seq_spec = pl.BlockSpec((1, ts, cg), lambda bi, gi, ci: (bi, ci, gi))
    return pl.pallas_call(
        functools.partial(_rglru_kernel, ts=ts),
        grid=(b, groups, l // ts),
        in_specs=[
            seq_spec, seq_spec,
            pl.BlockSpec((CONV_WIDTH, nl, 1, LANES), lambda bi, gi, ci: (0, gi, 0, 0)),
            pl.BlockSpec((nl, 1, LANES), lambda bi, gi, ci: (gi, 0, 0)),
            pl.BlockSpec((1, cg, 2 * cg), lambda bi, gi, ci: (gi, 0, 0)),
            pl.BlockSpec((1, 1, 2 * cg), lambda bi, gi, ci: (gi, 0, 0)),
            pl.BlockSpec((1, cg), lambda bi, gi, ci: (0, gi)),
        ],
        out_specs=seq_spec,
        out_shape=jax.ShapeDtypeStruct((b, l, c), BF16),
        scratch_shapes=[
            pltpu.VMEM((nl, SUBLANES + ts, LANES), F32),
            pltpu.VMEM((ts, cg), F32),
            pltpu.VMEM((nl, ts, LANES), F32),
            pltpu.VMEM((n_runs, cg), F32),
            pltpu.VMEM((n_runs, cg), F32),
            pltpu.VMEM((n_runs, cg), F32),
            pltpu.VMEM((SUBLANES, cg), F32),
        ],
        compiler_params=_params("parallel", "parallel", "arbitrary"),
    )(xr, gate, conv_w.reshape(CONV_WIDTH, c // LANES, 1, LANES),
      conv_b.reshape(c // LANES, 1, LANES), w_gates, b_gates, lam.reshape(1, c))


def _gate_weights(w_r, b_r, w_i, b_i):
    nb, gb, _ = w_r.shape
    groups = nb // GATE_GROUP
    eye = jnp.eye(GATE_GROUP, dtype=w_r.dtype)

    def dense(w):
        w = w.reshape(groups, GATE_GROUP, gb, gb)
        w = jnp.einsum("gjcd,jk->gjckd", w, eye)
        return w.reshape(groups, GATE_GROUP * gb, GATE_GROUP * gb)

    w = 0.5 * jnp.concatenate([dense(w_r), dense(w_i)], axis=2)
    cg = GATE_GROUP * gb
    b = 0.5 * jnp.concatenate([b_r.reshape(groups, 1, cg), b_i.reshape(groups, 1, cg)], axis=2)
    return w.astype(BF16), b.astype(F32)


def _attn_kernel(lam_ref, g_ref, q_ref, k_ref, v_ref, o_ref, acc, *, tq, lambda_init):
    qi = pl.program_id(2)
    hw = 2 * HEAD_DIM
    n_chains = acc.shape[0]
    chains = range(n_chains)
    q = q_ref[0]
    q_parts = [q[:, c * HEAD_DIM:(c + 1) * HEAD_DIM] for c in chains]
    acc[...] = jnp.zeros(acc.shape, F32)

    def step(first_block, n_blocks, carry, masked):
        width = n_blocks * tq
        start = pl.multiple_of(first_block * tq, tq)
        k = k_ref[0, pl.ds(start, width), :]
        v = v_ref[0, pl.ds(start, width), :]
        s = [lax.dot_general(q_parts[c], k[:, c * HEAD_DIM:(c + 1) * HEAD_DIM],
                             (((1,), (1,)), ((), ())), preferred_element_type=F32)
             for c in chains]
        if masked:
            row = lax.broadcasted_iota(jnp.int32, s[0].shape, 0)
            col = lax.broadcasted_iota(jnp.int32, s[0].shape, 1)
            keep = col - (n_blocks - 1) * tq <= row
            s = [jnp.where(keep, s[c], -jnp.inf) for c in chains]
        m_new = [jnp.maximum(carry[2 * c], jnp.max(s[c], axis=-1, keepdims=True)) for c in chains]
        p = [jnp.exp2(s[c] - m_new[c]) for c in chains]
        alpha = [jnp.exp2(carry[2 * c] - m_new[c]) for c in chains]
        l_new = [alpha[c] * carry[2 * c + 1] + jnp.sum(p[c], axis=-1, keepdims=True)
                 for c in chains]
        pv = [jnp.dot(p[c].astype(BF16), v[:, (c // 2) * hw:(c // 2 + 1) * hw],
                      preferred_element_type=F32) for c in chains]
        for c in chains:
            acc[c] = alpha[c] * acc[c] + pv[c]
        out = []
        for c in chains:
            out += [m_new[c], l_new[c]]
        return tuple(out)

    neg = jnp.full((tq, 1), -jnp.inf, F32)
    zero = jnp.zeros((tq, 1), F32)
    n_wide = qi // KV_BLOCKS_PER_STEP
    carry = lax.fori_loop(
        0, n_wide,
        lambda i, cr: step(i * KV_BLOCKS_PER_STEP, KV_BLOCKS_PER_STEP, cr, False),
        (neg, zero) * n_chains)
    rest = qi - n_wide * KV_BLOCKS_PER_STEP
    tails = [functools.partial(step, qi - r, r + 1, masked=True)
             for r in range(KV_BLOCKS_PER_STEP)]
    carry = lax.switch(rest, tails, carry)

    lv = lam_ref[0]
    lam = (jnp.exp(jnp.sum(lv[0:1] * lv[1:2], axis=-1, keepdims=True))
           - jnp.exp(jnp.sum(lv[2:3] * lv[3:4], axis=-1, keepdims=True)) + lambda_init)
    for head in range(n_chains // 2):
        c0, c1 = 2 * head, 2 * head + 1
        o = acc[c0] / carry[2 * c0 + 1] - lam * (acc[c1] / carry[2 * c1 + 1])
        o = o * _rms_scale(o, SUBLN_EPS) * g_ref[...] * (1.0 - lambda_init)
        o_ref[0, :, head * hw:(head + 1) * hw] = o.astype(o_ref.dtype)


def _diff_attention(q, k, v, lam_vecs, subln_g, lambda_init):
    b, l, w = q.shape
    hw = 2 * HEAD_DIM
    bw = HEADS_PER_STEP * hw
    tq = SEQ_TILE
    kv_spec = pl.BlockSpec((1, l, bw), lambda bi, hi, qi: (bi, 0, hi))
    q_spec = pl.BlockSpec((1, tq, bw), lambda bi, hi, qi: (bi, qi, hi))
    return pl.pallas_call(
        functools.partial(_attn_kernel, tq=tq, lambda_init=lambda_init),
        grid=(b, w // bw, l // tq),
        in_specs=[
            pl.BlockSpec((1, 4, HEAD_DIM), lambda bi, hi, qi: (0, 0, 0)),
            pl.BlockSpec((1, hw), lambda bi, hi, qi: (0, 0)),
            q_spec, kv_spec, kv_spec,
        ],
        out_specs=q_spec,
        out_shape=jax.ShapeDtypeStruct((b, l, w), BF16),
        scratch_shapes=[pltpu.VMEM((2 * HEADS_PER_STEP, tq, hw), F32)],
        compiler_params=_params("parallel", "parallel", "arbitrary"),
    )(lam_vecs.reshape(1, 4, HEAD_DIM), subln_g.reshape(1, hw), q, k, v)


def _rope_tables(length, batch):
    inv = 1.0 / (ROPE_THETA ** (jnp.arange(0, HEAD_DIM, 2, dtype=F32) / HEAD_DIM))
    ang = jnp.arange(length, dtype=F32)[:, None] * inv[None, :]
    cos, sin = jnp.cos(ang), jnp.sin(ang)
    cos2 = jnp.concatenate([cos, cos], axis=1)
    sin2 = jnp.concatenate([-sin, sin], axis=1)
    return jnp.tile(cos2, (batch, 1)), jnp.tile(sin2, (batch, 1))


def kernel(x, meta_tokens, a_norm_g, a_w_in, a_conv_w, a_conv_b, a_w_r, a_b_r, a_w_i, a_b_i,
           a_lambda, a_w_out, kv_norm_g, w_kv, b_norm_g, b_w_q, b_lambda, b_subln_g, b_w_o,
           mlp_norm_g, mlp_w1, mlp_w2, final_norm_g):
    batch, seq, d = x.shape
    n_a = a_w_in.shape[0]
    depth = mlp_w1.shape[0]
    d_rnn = a_w_out.shape[1]
    qk_width = b_w_q.shape[2]
    v_width = w_kv.shape[1] - qk_width
    length = N_META_TOKENS + seq
    lp = pl.cdiv(length, SEQ_TILE) * SEQ_TILE
    t = batch * lp
    assert t % TOKEN_TILE == 0

    meta = jnp.broadcast_to(meta_tokens[None].astype(x.dtype), (batch, N_META_TOKENS, d))
    pad = jnp.zeros((batch, lp - length, d), x.dtype)
    h = jnp.concatenate([meta, x, pad], axis=1).reshape(t, d)
    rope = _rope_tables(lp, batch)

    k_sh = v_sh = None
    for layer in range(depth):
        if layer < n_a:
            j = layer
            w_in = a_w_in[j].astype(BF16)
            gate = _norm_matmul(h, a_norm_g[j], w_in, 0, d_rnn, BF16, act="gelu")
            xr = _norm_matmul(h, a_norm_g[j], w_in, d_rnn, d_rnn, F32)
            w_g, b_g = _gate_weights(a_w_r[j], a_b_r[j], a_w_i[j], a_b_i[j])
            y = _rglru(xr.reshape(batch, lp, d_rnn), gate.reshape(batch, lp, d_rnn),
                       a_conv_w[j], a_conv_b[j], w_g, b_g, a_lambda[j])
            h = _matmul_residual(y.reshape(t, d_rnn), a_w_out[j].astype(BF16), h)
        else:
            j = layer - n_a
            if j == 0:
                wkv = w_kv.astype(BF16)
                k_sh = _norm_matmul(h, kv_norm_g, wkv, 0, qk_width, BF16, rope=rope)
                v_sh = _norm_matmul(h, kv_norm_g, wkv, qk_width, v_width, BF16)
                k_sh = k_sh.reshape(batch, lp, qk_width)
                v_sh = v_sh.reshape(batch, lp, v_width)
            lambda_init = 0.8 - 0.6 * math.exp(-0.3 * layer)
            q = _norm_matmul(h, b_norm_g[j], b_w_q[j].astype(BF16), 0, qk_width, BF16,
                             rope=rope, scale=math.log2(math.e) / math.sqrt(HEAD_DIM))
            o = _diff_attention(q.reshape(batch, lp, qk_width), k_sh, v_sh, b_lambda[j],
                                b_subln_g[j], lambda_init)
            h = _matmul_residual(o.reshape(t, v_width), b_w_o[j].astype(BF16), h)
        final_g = final_norm_g if layer == depth - 1 else None
        h = _mlp(h, mlp_norm_g[layer], mlp_w1[layer].astype(BF16), mlp_w2[layer].astype(BF16),
                 final_g)
    return h.reshape(batch, lp, d)[:, N_META_TOKENS:length]
```

```python
import functools
import math

import jax
import jax.numpy as jnp
from jax import lax
from jax.experimental import pallas as pl
from jax.experimental.pallas import tpu as pltpu

F32 = jnp.float32
BF16 = jnp.bfloat16

N_META_TOKENS = 16
CONV_WIDTH = 4
LRU_C = 8.0
HEAD_DIM = 128
ROPE_THETA = 10000.0
NORM_EPS = 1e-6
SUBLN_EPS = 1e-5

LANES = 128
SUBLANES = 8
VMEM_LIMIT_BYTES = 56 * 1024 * 1024

SEQ_TILE = 384
TOKEN_TILE = 768
COL_TILES = (1280, 1024, 512)
FF_TILE = 512
GATE_GROUP = 4
SCAN_ROWS = 264
RGLRU_TILE = 528
KV_BLOCKS_PER_STEP = 2
HEADS_PER_STEP = 2


def _params(*semantics):
    return pltpu.CompilerParams(dimension_semantics=semantics,
                                vmem_limit_bytes=VMEM_LIMIT_BYTES)


def _col_tile(n):
    return next(t for t in COL_TILES if n % t == 0)


def _rms_scale(x, eps):
    return lax.rsqrt(jnp.mean(x * x, axis=-1, keepdims=True) + eps)


def _norm_matmul_kernel(*refs, act, rope, scale):
    if rope:
        x_ref, g_ref, w_ref, cos_ref, sin_ref, o_ref, xn_ref = refs
    else:
        x_ref, g_ref, w_ref, o_ref, xn_ref = refs

    @pl.when(pl.program_id(1) == 0)
    def _():
        x = x_ref[...]
        xn_ref[...] = (x * _rms_scale(x, NORM_EPS) * g_ref[...]).astype(xn_ref.dtype)

    y = jnp.dot(xn_ref[...], w_ref[...], preferred_element_type=F32)
    if act == "gelu":
        y = jax.nn.gelu(y)
    if rope:
        c = cos_ref[...]
        s = sin_ref[...]
        segs = []
        for i in range(y.shape[1] // HEAD_DIM):
            seg = y[:, i * HEAD_DIM:(i + 1) * HEAD_DIM]
            segs.append(seg * c + pltpu.roll(seg, HEAD_DIM // 2, axis=1) * s)
        y = jnp.concatenate(segs, axis=1)
    if scale != 1.0:
        y = y * scale
    o_ref[...] = y.astype(o_ref.dtype)


def _norm_matmul(x, g, w, layer, col0, n, out_dtype, act=None, rope=None, scale=1.0):
    t, d = x.shape
    tm, tn = TOKEN_TILE, _col_tile(n)
    assert col0 % tn == 0
    j0 = col0 // tn
    in_specs = [
        pl.BlockSpec((tm, d), lambda i, j: (i, 0)),
        pl.BlockSpec((1, d), lambda i, j: (0, 0)),
        pl.BlockSpec((None, d, tn), lambda i, j: (layer, 0, j0 + j)),
    ]
    args = [x, g.reshape(1, d), w]
    if rope is not None:
        in_specs += [pl.BlockSpec((tm, HEAD_DIM), lambda i, j: (i, 0))] * 2
        args += list(rope)
    return pl.pallas_call(
        functools.partial(_norm_matmul_kernel, act=act, rope=rope is not None, scale=scale),
        grid=(t // tm, n // tn),
        in_specs=in_specs,
        out_specs=pl.BlockSpec((tm, tn), lambda i, j: (i, j)),
        out_shape=jax.ShapeDtypeStruct((t, n), out_dtype),
        scratch_shapes=[pltpu.VMEM((tm, d), BF16)],
        compiler_params=_params("parallel", "arbitrary"),
    )(*args)


def _matmul_residual_kernel(y_ref, w_ref, r_ref, o_ref):
    o_ref[...] = r_ref[...] + jnp.dot(y_ref[...], w_ref[...], preferred_element_type=F32)


def _matmul_residual(y, w, layer, res):
    t, k = y.shape
    n = w.shape[2]
    tm, tn = TOKEN_TILE, _col_tile(n)
    return pl.pallas_call(
        _matmul_residual_kernel,
        grid=(t // tm, n // tn),
        in_specs=[
            pl.BlockSpec((tm, k), lambda i, j: (i, 0)),
            pl.BlockSpec((None, k, tn), lambda i, j: (layer, 0, j)),
            pl.BlockSpec((tm, tn), lambda i, j: (i, j)),
        ],
        out_specs=pl.BlockSpec((tm, tn), lambda i, j: (i, j)),
        out_shape=jax.ShapeDtypeStruct((t, n), F32),
        compiler_params=_params("parallel", "arbitrary"),
    )(y, w, res)


def _mlp_kernel(*refs, final):
    if final:
        x_ref, g_ref, w1_ref, w2_ref, fg_ref, o_ref, xn_ref = refs
    else:
        x_ref, g_ref, w1_ref, w2_ref, o_ref, xn_ref = refs
    j = pl.program_id(1)

    @pl.when(j == 0)
    def _():
        x = x_ref[...]
        xn_ref[...] = (x * _rms_scale(x, NORM_EPS) * g_ref[...]).astype(xn_ref.dtype)
        o_ref[...] = x

    h = jnp.dot(xn_ref[...], w1_ref[...].astype(BF16), preferred_element_type=F32)
    h = jnp.square(jnp.maximum(h, 0.0)).astype(BF16)
    o_ref[...] += jnp.dot(h, w2_ref[...].astype(BF16), preferred_element_type=F32)

    if final:
        @pl.when(j == pl.num_programs(1) - 1)
        def _():
            y = o_ref[...]
            o_ref[...] = y * _rms_scale(y, NORM_EPS) * fg_ref[...]


def _mlp(x, g, w1, w2, layer, final_g=None):
    t, d = x.shape
    f = w1.shape[2]
    tm, tf = TOKEN_TILE, FF_TILE
    in_specs = [
        pl.BlockSpec((tm, d), lambda i, j: (i, 0)),
        pl.BlockSpec((1, d), lambda i, j: (0, 0)),
        pl.BlockSpec((None, d, tf), lambda i, j: (layer, 0, j)),
        pl.BlockSpec((None, tf, d), lambda i, j: (layer, j, 0)),
    ]
    args = [x, g.reshape(1, d), w1, w2]
    if final_g is not None:
        in_specs.append(pl.BlockSpec((1, d), lambda i, j: (0, 0)))
        args.append(final_g.reshape(1, d))
    return pl.pallas_call(
        functools.partial(_mlp_kernel, final=final_g is not None),
        grid=(t // tm, f // tf),
        in_specs=in_specs,
        out_specs=pl.BlockSpec((tm, d), lambda i, j: (i, 0)),
        out_shape=jax.ShapeDtypeStruct((t, d), F32),
        scratch_shapes=[pltpu.VMEM((tm, d), BF16)],
        compiler_params=_params("parallel", "arbitrary"),
    )(*args)


def _rglru_kernel(xr_ref, gate_ref, cw_ref, cb_ref, w_ref, b_ref, lam_ref, y_ref,
                  xl, xc_scr, hl, aend, bend, cin, hcar, *, ts):
    c = xr_ref.shape[2]
    n_lane_tiles = c // LANES
    lane_tiles = range(n_lane_tiles)
    seg = SCAN_ROWS // SUBLANES
    n_groups = ts // SCAN_ROWS
    hdr = SUBLANES

    @pl.when(pl.program_id(2) == 0)
    def _():
        xl[:, 0:hdr, :] = jnp.zeros((n_lane_tiles, hdr, LANES), F32)
        hcar[...] = jnp.zeros(hcar.shape, F32)

    for j in lane_tiles:
        xl[j, hdr:hdr + ts, :] = xr_ref[0, :, j * LANES:(j + 1) * LANES]

    def lane(x, j):
        return x[:, j * LANES:(j + 1) * LANES]

    taps = [cw_ref[tap] for tap in range(CONV_WIDTH)]
    bias = cb_ref[...]
    for g in range(n_groups):
        loads = {}
        for k in range(-(CONV_WIDTH - 1), seg):
            loads[k] = xl[:, pl.ds(hdr + g * SCAN_ROWS + k, SUBLANES, stride=seg), :]
        for k in range(seg):
            xc = loads[k] * taps[CONV_WIDTH - 1] + bias
            for back in range(1, CONV_WIDTH):
                xc = xc + loads[k - back] * taps[CONV_WIDTH - 1 - back]
            r0 = g * SCAN_ROWS + k * SUBLANES
            for j in lane_tiles:
                xc_scr[r0:r0 + SUBLANES, j * LANES:(j + 1) * LANES] = xc[j]
    xl[:, 0:hdr, :] = xl[:, ts:ts + hdr, :]

    xc = xc_scr[...]
    half_gates = jnp.dot(xc.astype(BF16), w_ref[0], preferred_element_type=F32) + b_ref[0]
    t_r = jnp.tanh(half_gates[:, :c])
    t_i = jnp.tanh(half_gates[:, c:])
    half_rate = (-0.5 * LRU_C * math.log2(math.e)) * jax.nn.softplus(-lam_ref[...])
    a = jnp.exp2(t_r * half_rate + half_rate)
    bv = jnp.exp(0.5 * jnp.log(1.0 - a * a)) * ((0.5 * t_i + 0.5) * xc)

    def rows(x, g, k):
        r0 = g * SCAN_ROWS + k * SUBLANES
        return x[r0:r0 + SUBLANES, :]

    for g in range(n_groups):
        a_cum, b_cum = rows(a, g, 0), rows(bv, g, 0)
        for k in range(1, seg):
            a_k = rows(a, g, k)
            b_cum = a_k * b_cum + rows(bv, g, k)
            a_cum = a_k * a_cum
        aend[g * SUBLANES:(g + 1) * SUBLANES, :] = a_cum
        bend[g * SUBLANES:(g + 1) * SUBLANES, :] = b_cum

    carry = hcar[0:1, :]
    for s in range(n_groups * SUBLANES):
        cin[s:s + 1, :] = carry
        carry = aend[s:s + 1, :] * carry + bend[s:s + 1, :]
    hcar[0:1, :] = carry

    for g in range(n_groups):
        h = cin[g * SUBLANES:(g + 1) * SUBLANES, :]
        for k in range(seg):
            h = rows(a, g, k) * h + rows(bv, g, k)
            for j in lane_tiles:
                hl[j, pl.ds(g * SCAN_ROWS + k, SUBLANES, stride=seg), :] = lane(h, j)
    for j in lane_tiles:
        lanes = slice(j * LANES, (j + 1) * LANES)
        y_ref[0, :, lanes] = (hl[j] * gate_ref[0, :, lanes].astype(F32)).astype(y_ref.dtype)


def _rglru(xr, gate, conv_w, conv_b, w_gates, b_gates, lam):
    b, l, c = xr.shape
    groups = w_gates.shape[0]
    cg = c // groups
    nl = cg // LANES
    ts = RGLRU_TILE
    assert l % ts == 0
    n_runs = ts // SCAN_ROWS * SUBLANES
    seq_spec = pl.BlockSpec((1, ts, cg), lambda bi, gi, ci: (bi, ci, gi))
    return pl.pallas_call(
        functools.partial(_rglru_kernel, ts=ts),
        grid=(b, groups, l // ts),
        in_specs=[
            seq_spec, seq_spec,
            pl.BlockSpec((CONV_WIDTH, nl, 1, LANES), lambda bi, gi, ci: (0, gi, 0, 0)),
            pl.BlockSpec((nl, 1, LANES), lambda bi, gi, ci: (gi, 0, 0)),
            pl.BlockSpec((1, cg, 2 * cg), lambda bi, gi, ci: (gi, 0, 0)),
            pl.BlockSpec((1, 1, 2 * cg), lambda bi, gi, ci: (gi, 0, 0)),
            pl.BlockSpec((1, cg), lambda bi, gi, ci: (0, gi)),
        ],
        out_specs=seq_spec,
        out_shape=jax.ShapeDtypeStruct((b, l, c), BF16),
        scratch_shapes=[
            pltpu.VMEM((nl, SUBLANES + ts, LANES), F32),
            pltpu.VMEM((ts, cg), F32),
            pltpu.VMEM((nl, ts, LANES), F32),
            pltpu.VMEM((n_runs, cg), F32),
            pltpu.VMEM((n_runs, cg), F32),
            pltpu.VMEM((n_runs, cg), F32),
            pltpu.VMEM((SUBLANES, cg), F32),
        ],
        compiler_params=_params("parallel", "parallel", "arbitrary"),
    )(xr, gate, conv_w.reshape(CONV_WIDTH, c // LANES, 1, LANES),
      conv_b.reshape(c // LANES, 1, LANES), w_gates, b_gates, lam.reshape(1, c))


def _gate_weights(w_r, b_r, w_i, b_i):
    nb, gb, _ = w_r.shape
    groups = nb // GATE_GROUP
    eye = jnp.eye(GATE_GROUP, dtype=w_r.dtype)

    def dense(w):
        w = w.reshape(groups, GATE_GROUP, gb, gb)
        w = jnp.einsum("gjcd,jk->gjckd", w, eye)
        return w.reshape(groups, GATE_GROUP * gb, GATE_GROUP * gb)

    w = 0.5 * jnp.concatenate([dense(w_r), dense(w_i)], axis=2)
    cg = GATE_GROUP * gb
    b = 0.5 * jnp.concatenate([b_r.reshape(groups, 1, cg), b_i.reshape(groups, 1, cg)], axis=2)
    return w.astype(BF16), b.astype(F32)


def _attn_kernel(lam_ref, g_ref, q_ref, k_ref, v_ref, o_ref, acc, *, tq, lambda_init):
    qi = pl.program_id(2)
    hw = 2 * HEAD_DIM
    n_chains = acc.shape[0]
    chains = range(n_chains)
    q = q_ref[0]
    q_parts = [q[:, c * HEAD_DIM:(c + 1) * HEAD_DIM] for c in chains]
    acc[...] = jnp.zeros(acc.shape, F32)

    def step(first_block, n_blocks, carry, masked):
        width = n_blocks * tq
        start = pl.multiple_of(first_block * tq, tq)
        k = k_ref[0, pl.ds(start, width), :]
        v = v_ref[0, pl.ds(start, width), :]
        s = [lax.dot_general(q_parts[c], k[:, c * HEAD_DIM:(c + 1) * HEAD_DIM],
                             (((1,), (1,)), ((), ())), preferred_element_type=F32)
             for c in chains]
        if masked:
            row = lax.broadcasted_iota(jnp.int32, s[0].shape, 0)
            col = lax.broadcasted_iota(jnp.int32, s[0].shape, 1)
            keep = col - (n_blocks - 1) * tq <= row
            s = [jnp.where(keep, s[c], -jnp.inf) for c in chains]
        m_new = [jnp.maximum(carry[2 * c], jnp.max(s[c], axis=-1, keepdims=True)) for c in chains]
        p = [jnp.exp2(s[c] - m_new[c]) for c in chains]
        alpha = [jnp.exp2(carry[2 * c] - m_new[c]) for c in chains]
        l_new = [alpha[c] * carry[2 * c + 1] + jnp.sum(p[c], axis=-1, keepdims=True)
                 for c in chains]
        pv = [jnp.dot(p[c].astype(BF16), v[:, (c // 2) * hw:(c // 2 + 1) * hw],
                      preferred_element_type=F32) for c in chains]
        for c in chains:
            acc[c] = alpha[c] * acc[c] + pv[c]
        out = []
        for c in chains:
            out += [m_new[c], l_new[c]]
        return tuple(out)

    neg = jnp.full((tq, 1), -jnp.inf, F32)
    zero = jnp.zeros((tq, 1), F32)
    n_wide = qi // KV_BLOCKS_PER_STEP
    carry = lax.fori_loop(
        0, n_wide,
        lambda i, cr: step(i * KV_BLOCKS_PER_STEP, KV_BLOCKS_PER_STEP, cr, False),
        (neg, zero) * n_chains)
    rest = qi - n_wide * KV_BLOCKS_PER_STEP
    tails = [functools.partial(step, qi - r, r + 1, masked=True)
             for r in range(KV_BLOCKS_PER_STEP)]
    carry = lax.switch(rest, tails, carry)

    lv = lam_ref[0]
    lam = (jnp.exp(jnp.sum(lv[0:1] * lv[1:2], axis=-1, keepdims=True))
           - jnp.exp(jnp.sum(lv[2:3] * lv[3:4], axis=-1, keepdims=True)) + lambda_init)
    for head in range(n_chains // 2):
        c0, c1 = 2 * head, 2 * head + 1
        o = acc[c0] / carry[2 * c0 + 1] - lam * (acc[c1] / carry[2 * c1 + 1])
        o = o * _rms_scale(o, SUBLN_EPS) * g_ref[...] * (1.0 - lambda_init)
        o_ref[0, :, head * hw:(head + 1) * hw] = o.astype(o_ref.dtype)


def _diff_attention(q, k, v, lam_vecs, subln_g, lambda_init):
    b, l, w = q.shape
    hw = 2 * HEAD_DIM
    bw = HEADS_PER_STEP * hw
    tq = SEQ_TILE
    kv_spec = pl.BlockSpec((1, l, bw), lambda bi, hi, qi: (bi, 0, hi))
    q_spec = pl.BlockSpec((1, tq, bw), lambda bi, hi, qi: (bi, qi, hi))
    return pl.pallas_call(
        functools.partial(_attn_kernel, tq=tq, lambda_init=lambda_init),
        grid=(b, w // bw, l // tq),
        in_specs=[
            pl.BlockSpec((1, 4, HEAD_DIM), lambda bi, hi, qi: (0, 0, 0)),
            pl.BlockSpec((1, hw), lambda bi, hi, qi: (0, 0)),
            q_spec, kv_spec, kv_spec,
        ],
        out_specs=q_spec,
        out_shape=jax.ShapeDtypeStruct((b, l, w), BF16),
        scratch_shapes=[pltpu.VMEM((2 * HEADS_PER_STEP, tq, hw), F32)],
        compiler_params=_params("parallel", "parallel", "arbitrary"),
    )(lam_vecs.reshape(1, 4, HEAD_DIM), subln_g.reshape(1, hw), q, k, v)


def _rope_tables(length, batch):
    inv = 1.0 / (ROPE_THETA ** (jnp.arange(0, HEAD_DIM, 2, dtype=F32) / HEAD_DIM))
    ang = jnp.arange(length, dtype=F32)[:, None] * inv[None, :]
    cos, sin = jnp.cos(ang), jnp.sin(ang)
    cos2 = jnp.concatenate([cos, cos], axis=1)
    sin2 = jnp.concatenate([-sin, sin], axis=1)
    return jnp.tile(cos2, (batch, 1)), jnp.tile(sin2, (batch, 1))


def kernel(x, meta_tokens, a_norm_g, a_w_in, a_conv_w, a_conv_b, a_w_r, a_b_r, a_w_i, a_b_i,
           a_lambda, a_w_out, kv_norm_g, w_kv, b_norm_g, b_w_q, b_lambda, b_subln_g, b_w_o,
           mlp_norm_g, mlp_w1, mlp_w2, final_norm_g):
    batch, seq, d = x.shape
    n_a = a_w_in.shape[0]
    depth = mlp_w1.shape[0]
    d_rnn = a_w_out.shape[1]
    qk_width = b_w_q.shape[2]
    v_width = w_kv.shape[1] - qk_width
    length = N_META_TOKENS + seq
    lp = pl.cdiv(length, SEQ_TILE) * SEQ_TILE
    t = batch * lp
    assert t % TOKEN_TILE == 0

    meta = jnp.broadcast_to(meta_tokens[None].astype(x.dtype), (batch, N_META_TOKENS, d))
    pad = jnp.zeros((batch, lp - length, d), x.dtype)
    h = jnp.concatenate([meta, x, pad], axis=1).reshape(t, d)
    rope = _rope_tables(lp, batch)

    w_in, w_out = a_w_in.astype(BF16), a_w_out.astype(BF16)
    wkv, w_q, w_o = w_kv.astype(BF16)[None], b_w_q.astype(BF16), b_w_o.astype(BF16)
    q_scale = math.log2(math.e) / math.sqrt(HEAD_DIM)

    k_sh = v_sh = None
    for layer in range(depth):
        if layer < n_a:
            j = layer
            gate = _norm_matmul(h, a_norm_g[j], w_in, j, 0, d_rnn, BF16, act="gelu")
            xr = _norm_matmul(h, a_norm_g[j], w_in, j, d_rnn, d_rnn, F32)
            w_g, b_g = _gate_weights(a_w_r[j], a_b_r[j], a_w_i[j], a_b_i[j])
            y = _rglru(xr.reshape(batch, lp, d_rnn), gate.reshape(batch, lp, d_rnn),
                       a_conv_w[j], a_conv_b[j], w_g, b_g, a_lambda[j])
            h = _matmul_residual(y.reshape(t, d_rnn), w_out, j, h)
        else:
            j = layer - n_a
            if j == 0:
                k_sh = _norm_matmul(h, kv_norm_g, wkv, 0, 0, qk_width, BF16, rope=rope)
                v_sh = _norm_matmul(h, kv_norm_g, wkv, 0, qk_width, v_width, BF16)
                k_sh = k_sh.reshape(batch, lp, qk_width)
                v_sh = v_sh.reshape(batch, lp, v_width)
            lambda_init = 0.8 - 0.6 * math.exp(-0.3 * layer)
            q = _norm_matmul(h, b_norm_g[j], w_q, j, 0, qk_width, BF16, rope=rope, scale=q_scale)
            o = _diff_attention(q.reshape(batch, lp, qk_width), k_sh, v_sh, b_lambda[j],
                                b_subln_g[j], lambda_init)
            h = _matmul_residual(o.reshape(t, v_width), w_o, j, h)
        final_g = final_norm_g if layer == depth - 1 else None
        h = _mlp(h, mlp_norm_g[layer], mlp_w1, mlp_w2, layer, final_g)
    return h.reshape(batch, lp, d)[:, N_META_TOKENS:length]
```

```python
import functools
import math

import jax
import jax.numpy as jnp
from jax import lax
from jax.experimental import pallas as pl
from jax.experimental.pallas import tpu as pltpu

F32 = jnp.float32
BF16 = jnp.bfloat16

N_META_TOKENS = 16
CONV_WIDTH = 4
LRU_C = 8.0
HEAD_DIM = 128
ROPE_THETA = 10000.0
NORM_EPS = 1e-6
SUBLN_EPS = 1e-5

LANES = 128
SUBLANES = 8
VMEM_LIMIT_BYTES = 56 * 1024 * 1024

SEQ_TILE = 384
TOKEN_TILE = 768
COL_TILES = (1280, 1024, 512)
FF_TILE = 512
GATE_GROUP = 4
SCAN_ROWS = 264
RGLRU_TILE = 528
KV_BLOCKS_PER_STEP = 2
HEADS_PER_STEP = 2

def _params(*semantics):
    return pltpu.CompilerParams(dimension_semantics=semantics,
                                vmem_limit_bytes=VMEM_LIMIT_BYTES)


def _col_tile(n):
    return next(t for t in COL_TILES if n % t == 0)


def _rms_scale(x, eps):
    return lax.rsqrt(jnp.mean(x * x, axis=-1, keepdims=True) + eps)


def _norm_matmul_kernel(*refs, act, rope, scale):
    if rope:
        x_ref, g_ref, w_ref, cos_ref, sin_ref, o_ref = refs
    else:
        x_ref, g_ref, w_ref, o_ref = refs

    x = x_ref[...]
    xn = (x * _rms_scale(x, NORM_EPS) * g_ref[...]).astype(BF16)
    y = jnp.dot(xn, w_ref[...], preferred_element_type=F32)
    if act == "gelu":
        y = jax.nn.gelu(y)
    if rope:
        c = cos_ref[...]
        s = sin_ref[...]
        segs = []
        for i in range(y.shape[1] // HEAD_DIM):
            seg = y[:, i * HEAD_DIM:(i + 1) * HEAD_DIM]
            segs.append(seg * c + pltpu.roll(seg, HEAD_DIM // 2, axis=1) * s)
        y = jnp.concatenate(segs, axis=1)
    if scale != 1.0:
        y = y * scale
    o_ref[...] = y.astype(o_ref.dtype)


def _norm_matmul(x, g, w, layer, col0, n, out_dtype, act=None, rope=None, scale=1.0):
    t, d = x.shape
    tm, tn = TOKEN_TILE, _col_tile(n)
    assert col0 % tn == 0
    j0 = col0 // tn
    in_specs = [
        pl.BlockSpec((tm, d), lambda i, j: (i, 0)),
        pl.BlockSpec((1, d), lambda i, j: (0, 0)),
        pl.BlockSpec((None, d, tn), lambda i, j: (layer, 0, j0 + j)),
    ]
    args = [x, g.reshape(1, d), w]
    if rope is not None:
        in_specs += [pl.BlockSpec((tm, HEAD_DIM), lambda i, j: (i, 0))] * 2
        args += list(rope)
    return pl.pallas_call(
        functools.partial(_norm_matmul_kernel, act=act, rope=rope is not None, scale=scale),
        grid=(t // tm, n // tn),
        in_specs=in_specs,
        out_specs=pl.BlockSpec((tm, tn), lambda i, j: (i, j)),
        out_shape=jax.ShapeDtypeStruct((t, n), out_dtype),
        compiler_params=_params("parallel", "parallel"),
    )(*args)


def _matmul_residual_kernel(y_ref, w_ref, r_ref, o_ref):
    o_ref[...] = r_ref[...] + jnp.dot(y_ref[...], w_ref[...], preferred_element_type=F32)


def _matmul_residual(y, w, layer, res):
    t, k = y.shape
    n = w.shape[2]
    tm, tn = TOKEN_TILE, _col_tile(n)
    return pl.pallas_call(
        _matmul_residual_kernel,
        grid=(t // tm, n // tn),
        in_specs=[
            pl.BlockSpec((tm, k), lambda i, j: (i, 0)),
            pl.BlockSpec((None, k, tn), lambda i, j: (layer, 0, j)),
            pl.BlockSpec((tm, tn), lambda i, j: (i, j)),
        ],
        out_specs=pl.BlockSpec((tm, tn), lambda i, j: (i, j)),
        out_shape=jax.ShapeDtypeStruct((t, n), F32),
        compiler_params=_params("parallel", "arbitrary"),
    )(y, w, res)


def _mlp_kernel(*refs, final):
    if final:
        x_ref, g_ref, w1_ref, w2_ref, fg_ref, o_ref, xn_ref = refs
    else:
        x_ref, g_ref, w1_ref, w2_ref, o_ref, xn_ref = refs
    j = pl.program_id(1)

    @pl.when(j == 0)
    def _():
        x = x_ref[...]
        xn_ref[...] = (x * _rms_scale(x, NORM_EPS) * g_ref[...]).astype(xn_ref.dtype)
        o_ref[...] = x

    h = jnp.dot(xn_ref[...], w1_ref[...].astype(BF16), preferred_element_type=F32)
    h = jnp.square(jnp.maximum(h, 0.0)).astype(BF16)
    o_ref[...] += jnp.dot(h, w2_ref[...].astype(BF16), preferred_element_type=F32)

    if final:
        @pl.when(j == pl.num_programs(1) - 1)
        def _():
            y = o_ref[...]
            o_ref[...] = y * _rms_scale(y, NORM_EPS) * fg_ref[...]


def _mlp(x, g, w1, w2, layer, final_g=None):
    t, d = x.shape
    f = w1.shape[2]
    tm, tf = TOKEN_TILE, FF_TILE
    in_specs = [
        pl.BlockSpec((tm, d), lambda i, j: (i, 0)),
        pl.BlockSpec((1, d), lambda i, j: (0, 0)),
        pl.BlockSpec((None, d, tf), lambda i, j: (layer, 0, j)),
        pl.BlockSpec((None, tf, d), lambda i, j: (layer, j, 0)),
    ]
    args = [x, g.reshape(1, d), w1, w2]
    if final_g is not None:
        in_specs.append(pl.BlockSpec((1, d), lambda i, j: (0, 0)))
        args.append(final_g.reshape(1, d))
    return pl.pallas_call(
        functools.partial(_mlp_kernel, final=final_g is not None),
        grid=(t // tm, f // tf),
        in_specs=in_specs,
        out_specs=pl.BlockSpec((tm, d), lambda i, j: (i, 0)),
        out_shape=jax.ShapeDtypeStruct((t, d), F32),
        scratch_shapes=[pltpu.VMEM((tm, d), BF16)],
        compiler_params=_params("parallel", "arbitrary"),
    )(*args)


def _rglru_kernel(xr_ref, gate_ref, cw_ref, cb_ref, w_ref, b_ref, lam_ref, y_ref,
                  xl, xc_scr, hl, aend, bend, cin, hcar, *, ts):
    c = xr_ref.shape[2]
    n_lane_tiles = c // LANES
    lane_tiles = range(n_lane_tiles)
    seg = SCAN_ROWS // SUBLANES
    n_groups = ts // SCAN_ROWS
    hdr = SUBLANES

    @pl.when(pl.program_id(2) == 0)
    def _():
        xl[:, 0:hdr, :] = jnp.zeros((n_lane_tiles, hdr, LANES), F32)
        hcar[...] = jnp.zeros(hcar.shape, F32)

    for j in lane_tiles:
        xl[j, hdr:hdr + ts, :] = xr_ref[0, :, j * LANES:(j + 1) * LANES]

    def lane(x, j):
        return x[:, j * LANES:(j + 1) * LANES]

    taps = [cw_ref[tap] for tap in range(CONV_WIDTH)]
    bias = cb_ref[...]
    for g in range(n_groups):
        loads = {}
        for k in range(-(CONV_WIDTH - 1), seg):
            loads[k] = xl[:, pl.ds(hdr + g * SCAN_ROWS + k, SUBLANES, stride=seg), :]
        for k in range(seg):
            xc = loads[k] * taps[CONV_WIDTH - 1] + bias
            for back in range(1, CONV_WIDTH):
                xc = xc + loads[k - back] * taps[CONV_WIDTH - 1 - back]
            r0 = g * SCAN_ROWS + k * SUBLANES
            for j in lane_tiles:
                xc_scr[r0:r0 + SUBLANES, j * LANES:(j + 1) * LANES] = xc[j]
    xl[:, 0:hdr, :] = xl[:, ts:ts + hdr, :]

    xc = xc_scr[...]
    half_gates = jnp.dot(xc.astype(BF16), w_ref[0], preferred_element_type=F32) + b_ref[0]
    t_r = jnp.tanh(half_gates[:, :c])
    t_i = jnp.tanh(half_gates[:, c:])
    half_rate = (-0.5 * LRU_C * math.log2(math.e)) * jax.nn.softplus(-lam_ref[...])
    a = jnp.exp2(t_r * half_rate + half_rate)
    bv = jnp.exp(0.5 * jnp.log(1.0 - a * a)) * ((0.5 * t_i + 0.5) * xc)

    def rows(x, g, k):
        r0 = g * SCAN_ROWS + k * SUBLANES
        return x[r0:r0 + SUBLANES, :]

    for g in range(n_groups):
        a_cum, b_cum = rows(a, g, 0), rows(bv, g, 0)
        for k in range(1, seg):
            a_k = rows(a, g, k)
            b_cum = a_k * b_cum + rows(bv, g, k)
            a_cum = a_k * a_cum
        aend[g * SUBLANES:(g + 1) * SUBLANES, :] = a_cum
        bend[g * SUBLANES:(g + 1) * SUBLANES, :] = b_cum

    carry = hcar[0:1, :]
    for s in range(n_groups * SUBLANES):
        cin[s:s + 1, :] = carry
        carry = aend[s:s + 1, :] * carry + bend[s:s + 1, :]
    hcar[0:1, :] = carry

    for g in range(n_groups):
        h = cin[g * SUBLANES:(g + 1) * SUBLANES, :]
        for k in range(seg):
            h = rows(a, g, k) * h + rows(bv, g, k)
            for j in lane_tiles:
                hl[j, pl.ds(g * SCAN_ROWS + k, SUBLANES, stride=seg), :] = lane(h, j)
    for j in lane_tiles:
        lanes = slice(j * LANES, (j + 1) * LANES)
        y_ref[0, :, lanes] = (hl[j] * gate_ref[0, :, lanes].astype(F32)).astype(y_ref.dtype)


def _rglru(xr, gate, conv_w, conv_b, w_gates, b_gates, lam):
    b, l, c = xr.shape
    groups = w_gates.shape[0]
    cg = c // groups
    nl = cg // LANES
    ts = RGLRU_TILE
    assert l % ts == 0
    n_runs = ts // SCAN_ROWS * SUBLANES
    seq_spec = pl.BlockSpec((1, ts, cg), lambda bi, gi, ci: (bi, ci, gi))
    return pl.pallas_call(
        functools.partial(_rglru_kernel, ts=ts),
        grid=(b, groups, l // ts),
        in_specs=[
            seq_spec, seq_spec,
            pl.BlockSpec((CONV_WIDTH, nl, 1, LANES), lambda bi, gi, ci: (0, gi, 0, 0)),
            pl.BlockSpec((nl, 1, LANES), lambda bi, gi, ci: (gi, 0, 0)),
            pl.BlockSpec((1, cg, 2 * cg), lambda bi, gi, ci: (gi, 0, 0)),
            pl.BlockSpec((1, 1, 2 * cg), lambda bi, gi, ci: (gi, 0, 0)),
            pl.BlockSpec((1, cg), lambda bi, gi, ci: (0, gi)),
        ],
        out_specs=seq_spec,
        out_shape=jax.ShapeDtypeStruct((b, l, c), BF16),
        scratch_shapes=[
            pltpu.VMEM((nl, SUBLANES + ts, LANES), F32),
            pltpu.VMEM((ts, cg), F32),
            pltpu.VMEM((nl, ts, LANES), F32),
            pltpu.VMEM((n_runs, cg), F32),
            pltpu.VMEM((n_runs, cg), F32),
            pltpu.VMEM((n_runs, cg), F32),
            pltpu.VMEM((SUBLANES, cg), F32),
        ],
        compiler_params=_params("parallel", "parallel", "arbitrary"),
    )(xr, gate, conv_w.reshape(CONV_WIDTH, c // LANES, 1, LANES),
      conv_b.reshape(c // LANES, 1, LANES), w_gates, b_gates, lam.reshape(1, c))


def _gate_weights(w_r, b_r, w_i, b_i):
    nb, gb, _ = w_r.shape
    groups = nb // GATE_GROUP
    eye = jnp.eye(GATE_GROUP, dtype=w_r.dtype)

    def dense(w):
        w = w.reshape(groups, GATE_GROUP, gb, gb)
        w = jnp.einsum("gjcd,jk->gjckd", w, eye)
        return w.reshape(groups, GATE_GROUP * gb, GATE_GROUP * gb)

    w = 0.5 * jnp.concatenate([dense(w_r), dense(w_i)], axis=2)
    cg = GATE_GROUP * gb
    b = 0.5 * jnp.concatenate([b_r.reshape(groups, 1, cg), b_i.reshape(groups, 1, cg)], axis=2)
    return w.astype(BF16), b.astype(F32)


def _attn_kernel(lam_ref, g_ref, q_ref, k_ref, v_ref, o_ref, acc, *, tq, lambda_init):
    qi = pl.program_id(2)
    hw = 2 * HEAD_DIM
    n_chains = acc.shape[0]
    chains = range(n_chains)
    q = q_ref[0]
    q_parts = [q[:, c * HEAD_DIM:(c + 1) * HEAD_DIM] for c in chains]
    acc[...] = jnp.zeros(acc.shape, F32)

    def step(first_block, n_blocks, carry, masked):
        width = n_blocks * tq
        start = pl.multiple_of(first_block * tq, tq)
        k = k_ref[0, pl.ds(start, width), :]
        v = v_ref[0, pl.ds(start, width), :]
        s = [lax.dot_general(q_parts[c], k[:, c * HEAD_DIM:(c + 1) * HEAD_DIM],
                             (((1,), (1,)), ((), ())), preferred_element_type=F32)
             for c in chains]
        if masked:
            row = lax.broadcasted_iota(jnp.int32, s[0].shape, 0)
            col = lax.broadcasted_iota(jnp.int32, s[0].shape, 1)
            keep = col - (n_blocks - 1) * tq <= row
            s = [jnp.where(keep, s[c], -jnp.inf) for c in chains]
        m_new = [jnp.maximum(carry[2 * c], jnp.max(s[c], axis=-1, keepdims=True)) for c in chains]
        p = [jnp.exp2(s[c] - m_new[c]) for c in chains]
        alpha = [jnp.exp2(carry[2 * c] - m_new[c]) for c in chains]
        l_new = [alpha[c] * carry[2 * c + 1] + jnp.sum(p[c], axis=-1, keepdims=True)
                 for c in chains]
        pv = [jnp.dot(p[c].astype(BF16), v[:, (c // 2) * hw:(c // 2 + 1) * hw],
                      preferred_element_type=F32) for c in chains]
        for c in chains:
            acc[c] = alpha[c] * acc[c] + pv[c]
        out = []
        for c in chains:
            out += [m_new[c], l_new[c]]
        return tuple(out)

    neg = jnp.full((tq, 1), -jnp.inf, F32)
    zero = jnp.zeros((tq, 1), F32)
    n_wide = qi // KV_BLOCKS_PER_STEP
    def wide(block, cr):
        return step(block, KV_BLOCKS_PER_STEP, cr, False)

    carry = lax.fori_loop(
        0, n_wide // 2,
        lambda i, cr: wide((2 * i + 1) * KV_BLOCKS_PER_STEP, wide(2 * i * KV_BLOCKS_PER_STEP, cr)),
        (neg, zero) * n_chains)
    carry = lax.cond(n_wide % 2 == 1,
                     lambda cr: wide((n_wide - 1) * KV_BLOCKS_PER_STEP, cr),
                     lambda cr: cr, carry)
    rest = qi - n_wide * KV_BLOCKS_PER_STEP
    tails = [functools.partial(step, qi - r, r + 1, masked=True)
             for r in range(KV_BLOCKS_PER_STEP)]
    carry = lax.switch(rest, tails, carry)

    lv = lam_ref[0]
    lam = (jnp.exp(jnp.sum(lv[0:1] * lv[1:2], axis=-1, keepdims=True))
           - jnp.exp(jnp.sum(lv[2:3] * lv[3:4], axis=-1, keepdims=True)) + lambda_init)
    for head in range(n_chains // 2):
        c0, c1 = 2 * head, 2 * head + 1
        o = acc[c0] * (1.0 / carry[2 * c0 + 1]) - acc[c1] * (lam / carry[2 * c1 + 1])
        o = o * _rms_scale(o, SUBLN_EPS) * g_ref[...] * (1.0 - lambda_init)
        o_ref[0, :, head * hw:(head + 1) * hw] = o.astype(o_ref.dtype)


def _diff_attention(q, k, v, lam_vecs, subln_g, lambda_init):
    b, l, w = q.shape
    hw = 2 * HEAD_DIM
    bw = HEADS_PER_STEP * hw
    n_chains = 2 * HEADS_PER_STEP
    tq = SEQ_TILE
    kv_spec = pl.BlockSpec((1, l, bw), lambda bi, hi, qi: (bi, 0, hi))
    q_spec = pl.BlockSpec((1, tq, bw), lambda bi, hi, qi: (bi, qi, hi))
    return pl.pallas_call(
        functools.partial(_attn_kernel, tq=tq, lambda_init=lambda_init),
        grid=(b, w // bw, l // tq),
        in_specs=[
            pl.BlockSpec((1, 4, HEAD_DIM), lambda bi, hi, qi: (0, 0, 0)),
            pl.BlockSpec((1, hw), lambda bi, hi, qi: (0, 0)),
            q_spec, kv_spec, kv_spec,
        ],
        out_specs=q_spec,
        out_shape=jax.ShapeDtypeStruct((b, l, w), BF16),
        scratch_shapes=[pltpu.VMEM((n_chains, tq, hw), F32)],
        compiler_params=_params("parallel", "parallel", "arbitrary"),
    )(lam_vecs.reshape(1, 4, HEAD_DIM), subln_g.reshape(1, hw), q, k, v)


def _rope_tables(length, batch):
    inv = 1.0 / (ROPE_THETA ** (jnp.arange(0, HEAD_DIM, 2, dtype=F32) / HEAD_DIM))
    ang = jnp.arange(length, dtype=F32)[:, None] * inv[None, :]
    cos, sin = jnp.cos(ang), jnp.sin(ang)
    cos2 = jnp.concatenate([cos, cos], axis=1)
    sin2 = jnp.concatenate([-sin, sin], axis=1)
    return jnp.tile(cos2, (batch, 1)), jnp.tile(sin2, (batch, 1))


def kernel(x, meta_tokens, a_norm_g, a_w_in, a_conv_w, a_conv_b, a_w_r, a_b_r, a_w_i, a_b_i,
           a_lambda, a_w_out, kv_norm_g, w_kv, b_norm_g, b_w_q, b_lambda, b_subln_g, b_w_o,
           mlp_norm_g, mlp_w1, mlp_w2, final_norm_g):
    batch, seq, d = x.shape
    n_a = a_w_in.shape[0]
    depth = mlp_w1.shape[0]
    d_rnn = a_w_out.shape[1]
    qk_width = b_w_q.shape[2]
    v_width = w_kv.shape[1] - qk_width
    length = N_META_TOKENS + seq
    lp = pl.cdiv(length, SEQ_TILE) * SEQ_TILE
    t = batch * lp
    assert t % TOKEN_TILE == 0

    meta = jnp.broadcast_to(meta_tokens[None].astype(x.dtype), (batch, N_META_TOKENS, d))
    pad = jnp.zeros((batch, lp - length, d), x.dtype)
    h = jnp.concatenate([meta, x, pad], axis=1).reshape(t, d)
    rope = _rope_tables(lp, batch)

    w_in, w_out = a_w_in.astype(BF16), a_w_out.astype(BF16)
    wkv, w_q, w_o = w_kv.astype(BF16)[None], b_w_q.astype(BF16), b_w_o.astype(BF16)
    q_scale = math.log2(math.e) / math.sqrt(HEAD_DIM)

    k_sh = v_sh = None
    for layer in range(depth):
        if layer < n_a:
            j = layer
            gate = _norm_matmul(h, a_norm_g[j], w_in, j, 0, d_rnn, BF16, act="gelu")
            xr = _norm_matmul(h, a_norm_g[j], w_in, j, d_rnn, d_rnn, F32)
            w_g, b_g = _gate_weights(a_w_r[j], a_b_r[j], a_w_i[j], a_b_i[j])
            y = _rglru(xr.reshape(batch, lp, d_rnn), gate.reshape(batch, lp, d_rnn),
                       a_conv_w[j], a_conv_b[j], w_g, b_g, a_lambda[j])
            h = _matmul_residual(y.reshape(t, d_rnn), w_out, j, h)
        else:
            j = layer - n_a
            if j == 0:
                k_sh = _norm_matmul(h, kv_norm_g, wkv, 0, 0, qk_width, BF16, rope=rope)
                v_sh = _norm_matmul(h, kv_norm_g, wkv, 0, qk_width, v_width, BF16)
                k_sh = k_sh.reshape(batch, lp, qk_width)
                v_sh = v_sh.reshape(batch, lp, v_width)
            lambda_init = 0.8 - 0.6 * math.exp(-0.3 * layer)
            q = _norm_matmul(h, b_norm_g[j], w_q, j, 0, qk_width, BF16, rope=rope, scale=q_scale)
            o = _diff_attention(q.reshape(batch, lp, qk_width), k_sh, v_sh, b_lambda[j],
                                b_subln_g[j], lambda_init)
            h = _matmul_residual(o.reshape(t, v_width), w_o, j, h)
        final_g = final_norm_g if layer == depth - 1 else None
        h = _mlp(h, mlp_norm_g[layer], mlp_w1, mlp_w2, layer, final_g)
    return h.reshape(batch, lp, d)[:, N_META_TOKENS:length]
```

```python
import functools
import math

import jax
import jax.numpy as jnp
from jax import lax
from jax.experimental import pallas as pl
from jax.experimental.pallas import tpu as pltpu

F32 = jnp.float32
BF16 = jnp.bfloat16

N_META_TOKENS = 16
CONV_WIDTH = 4
LRU_C = 8.0
HEAD_DIM = 128
ROPE_THETA = 10000.0
NORM_EPS = 1e-6
SUBLN_EPS = 1e-5

LANES = 128
SUBLANES = 8
VMEM_LIMIT_BYTES = 56 * 1024 * 1024

SEQ_TILE = 384
TOKEN_TILE = 768
COL_TILES = (1280, 1024, 512)
FF_TILE = 512
GATE_GROUP = 4
SCAN_ROWS = 264
RGLRU_TILE = 528
KV_BLOCKS_PER_STEP = 2
HEADS_PER_STEP = 2

def _params(*semantics):
    return pltpu.CompilerParams(dimension_semantics=semantics,
                                vmem_limit_bytes=VMEM_LIMIT_BYTES)


def _col_tile(n):
    return next(t for t in COL_TILES if n % t == 0)


def _rms_scale(x, eps):
    return lax.rsqrt(jnp.mean(x * x, axis=-1, keepdims=True) + eps)


def _norm_matmul_kernel(*refs, act, rope, scale):
    if rope:
        x_ref, g_ref, w_ref, cos_ref, sin_ref, o_ref = refs
    else:
        x_ref, g_ref, w_ref, o_ref = refs

    x = x_ref[...]
    xn = (x * _rms_scale(x, NORM_EPS) * g_ref[...]).astype(BF16)
    y = jnp.dot(xn, w_ref[...], preferred_element_type=F32)
    if act == "gelu":
        y = jax.nn.gelu(y)
    if rope:
        c = cos_ref[...]
        s = sin_ref[...]
        segs = []
        for i in range(y.shape[1] // HEAD_DIM):
            seg = y[:, i * HEAD_DIM:(i + 1) * HEAD_DIM]
            segs.append(seg * c + pltpu.roll(seg, HEAD_DIM // 2, axis=1) * s)
        y = jnp.concatenate(segs, axis=1)
    if scale != 1.0:
        y = y * scale
    o_ref[...] = y.astype(o_ref.dtype)


def _norm_matmul(x, g, w, layer, col0, n, out_dtype, act=None, rope=None, scale=1.0):
    t, d = x.shape
    tm = TOKEN_TILE
    assert col0 % n == 0
    j0 = col0 // n
    in_specs = [
        pl.BlockSpec((tm, d), lambda i: (i, 0)),
        pl.BlockSpec((1, d), lambda i: (0, 0)),
        pl.BlockSpec((None, d, n), lambda i: (layer, 0, j0), pipeline_mode=pl.Buffered(1)),
    ]
    args = [x, g.reshape(1, d), w]
    if rope is not None:
        in_specs += [pl.BlockSpec((tm, HEAD_DIM), lambda i: (i, 0))] * 2
        args += list(rope)
    return pl.pallas_call(
        functools.partial(_norm_matmul_kernel, act=act, rope=rope is not None, scale=scale),
        grid=(t // tm,),
        in_specs=in_specs,
        out_specs=pl.BlockSpec((tm, n), lambda i: (i, 0)),
        out_shape=jax.ShapeDtypeStruct((t, n), out_dtype),
        compiler_params=_params("parallel"),
    )(*args)


def _matmul_residual_kernel(y_ref, w_ref, r_ref, o_ref):
    o_ref[...] = r_ref[...] + jnp.dot(y_ref[...], w_ref[...], preferred_element_type=F32)


def _matmul_residual(y, w, layer, res):
    t, k = y.shape
    n = w.shape[2]
    tm = TOKEN_TILE
    return pl.pallas_call(
        _matmul_residual_kernel,
        grid=(t // tm,),
        in_specs=[
            pl.BlockSpec((tm, k), lambda i: (i, 0)),
            pl.BlockSpec((None, k, n), lambda i: (layer, 0, 0), pipeline_mode=pl.Buffered(1)),
            pl.BlockSpec((tm, n), lambda i: (i, 0)),
        ],
        out_specs=pl.BlockSpec((tm, n), lambda i: (i, 0)),
        out_shape=jax.ShapeDtypeStruct((t, n), F32),
        compiler_params=_params("parallel"),
    )(y, w, res)


def _mlp_kernel(*refs, final):
    if final:
        x_ref, g_ref, w1_ref, w2_ref, fg_ref, o_ref, xn_ref = refs
    else:
        x_ref, g_ref, w1_ref, w2_ref, o_ref, xn_ref = refs
    j = pl.program_id(1)

    @pl.when(j == 0)
    def _():
        x = x_ref[...]
        xn_ref[...] = (x * _rms_scale(x, NORM_EPS) * g_ref[...]).astype(xn_ref.dtype)
        o_ref[...] = x

    h = jnp.dot(xn_ref[...], w1_ref[...].astype(BF16), preferred_element_type=F32)
    h = jnp.square(jnp.maximum(h, 0.0)).astype(BF16)
    o_ref[...] += jnp.dot(h, w2_ref[...].astype(BF16), preferred_element_type=F32)

    if final:
        @pl.when(j == pl.num_programs(1) - 1)
        def _():
            y = o_ref[...]
            o_ref[...] = y * _rms_scale(y, NORM_EPS) * fg_ref[...]


def _mlp(x, g, w1, w2, layer, final_g=None):
    t, d = x.shape
    f = w1.shape[2]
    tm, tf = TOKEN_TILE, FF_TILE
    in_specs = [
        pl.BlockSpec((tm, d), lambda i, j: (i, 0)),
        pl.BlockSpec((1, d), lambda i, j: (0, 0)),
        pl.BlockSpec((None, d, tf), lambda i, j: (layer, 0, j)),
        pl.BlockSpec((None, tf, d), lambda i, j: (layer, j, 0)),
    ]
    args = [x, g.reshape(1, d), w1, w2]
    if final_g is not None:
        in_specs.append(pl.BlockSpec((1, d), lambda i, j: (0, 0)))
        args.append(final_g.reshape(1, d))
    return pl.pallas_call(
        functools.partial(_mlp_kernel, final=final_g is not None),
        grid=(t // tm, f // tf),
        in_specs=in_specs,
        out_specs=pl.BlockSpec((tm, d), lambda i, j: (i, 0)),
        out_shape=jax.ShapeDtypeStruct((t, d), F32),
        scratch_shapes=[pltpu.VMEM((tm, d), BF16)],
        compiler_params=_params("parallel", "arbitrary"),
    )(*args)


def _rglru_kernel(xr_ref, gate_ref, cw_ref, cb_ref, w_ref, b_ref, lam_ref, y_ref,
                  xl, xc_scr, hl, aend, bend, cin, hcar, *, ts):
    c = xr_ref.shape[2]
    n_lane_tiles = c // LANES
    lane_tiles = range(n_lane_tiles)
    seg = SCAN_ROWS // SUBLANES
    n_groups = ts // SCAN_ROWS
    hdr = SUBLANES

    @pl.when(pl.program_id(2) == 0)
    def _():
        xl[:, 0:hdr, :] = jnp.zeros((n_lane_tiles, hdr, LANES), F32)
        hcar[...] = jnp.zeros(hcar.shape, F32)

    for j in lane_tiles:
        xl[j, hdr:hdr + ts, :] = xr_ref[0, :, j * LANES:(j + 1) * LANES]

    def lane(x, j):
        return x[:, j * LANES:(j + 1) * LANES]

    taps = [cw_ref[tap] for tap in range(CONV_WIDTH)]
    bias = cb_ref[...]
    for g in range(n_groups):
        loads = {}
        for k in range(-(CONV_WIDTH - 1), seg):
            loads[k] = xl[:, pl.ds(hdr + g * SCAN_ROWS + k, SUBLANES, stride=seg), :]
        for k in range(seg):
            xc = loads[k] * taps[CONV_WIDTH - 1] + bias
            for back in range(1, CONV_WIDTH):
                xc = xc + loads[k - back] * taps[CONV_WIDTH - 1 - back]
            r0 = g * SCAN_ROWS + k * SUBLANES
            for j in lane_tiles:
                xc_scr[r0:r0 + SUBLANES, j * LANES:(j + 1) * LANES] = xc[j]
    xl[:, 0:hdr, :] = xl[:, ts:ts + hdr, :]

    xc = xc_scr[...]
    half_gates = jnp.dot(xc.astype(BF16), w_ref[0], preferred_element_type=F32) + b_ref[0]
    t_r = jnp.tanh(half_gates[:, :c])
    t_i = jnp.tanh(half_gates[:, c:])
    half_rate = (-0.5 * LRU_C * math.log2(math.e)) * jax.nn.softplus(-lam_ref[...])
    a = jnp.exp2(t_r * half_rate + half_rate)
    bv = jnp.exp(0.5 * jnp.log(1.0 - a * a)) * ((0.5 * t_i + 0.5) * xc)

    def rows(x, g, k):
        r0 = g * SCAN_ROWS + k * SUBLANES
        return x[r0:r0 + SUBLANES, :]

    for g in range(n_groups):
        a_cum, b_cum = rows(a, g, 0), rows(bv, g, 0)
        for k in range(1, seg):
            a_k = rows(a, g, k)
            b_cum = a_k * b_cum + rows(bv, g, k)
            a_cum = a_k * a_cum
        aend[g * SUBLANES:(g + 1) * SUBLANES, :] = a_cum
        bend[g * SUBLANES:(g + 1) * SUBLANES, :] = b_cum

    carry = hcar[0:1, :]
    for s in range(n_groups * SUBLANES):
        cin[s:s + 1, :] = carry
        carry = aend[s:s + 1, :] * carry + bend[s:s + 1, :]
    hcar[0:1, :] = carry

    for g in range(n_groups):
        h = cin[g * SUBLANES:(g + 1) * SUBLANES, :]
        for k in range(seg):
            h = rows(a, g, k) * h + rows(bv, g, k)
            for j in lane_tiles:
                hl[j, pl.ds(g * SCAN_ROWS + k, SUBLANES, stride=seg), :] = lane(h, j)
    for j in lane_tiles:
        lanes = slice(j * LANES, (j + 1) * LANES)
        y_ref[0, :, lanes] = (hl[j] * gate_ref[0, :, lanes].astype(F32)).astype(y_ref.dtype)


def _rglru(xr, gate, conv_w, conv_b, w_gates, b_gates, lam):
    b, l, c = xr.shape
    groups = w_gates.shape[0]
    cg = c // groups
    nl = cg // LANES
    ts = RGLRU_TILE
    assert l % ts == 0
    n_runs = ts // SCAN_ROWS * SUBLANES
    seq_spec = pl.BlockSpec((1, ts, cg), lambda bi, gi, ci: (bi, ci, gi))
    return pl.pallas_call(
        functools.partial(_rglru_kernel, ts=ts),
        grid=(b, groups, l // ts),
        in_specs=[
            seq_spec, seq_spec,
            pl.BlockSpec((CONV_WIDTH, nl, 1, LANES), lambda bi, gi, ci: (0, gi, 0, 0)),
            pl.BlockSpec((nl, 1, LANES), lambda bi, gi, ci: (gi, 0, 0)),
            pl.BlockSpec((1, cg, 2 * cg), lambda bi, gi, ci: (gi, 0, 0)),
            pl.BlockSpec((1, 1, 2 * cg), lambda bi, gi, ci: (gi, 0, 0)),
            pl.BlockSpec((1, cg), lambda bi, gi, ci: (0, gi)),
        ],
        out_specs=seq_spec,
        out_shape=jax.ShapeDtypeStruct((b, l, c), BF16),
        scratch_shapes=[
            pltpu.VMEM((nl, SUBLANES + ts, LANES), F32),
            pltpu.VMEM((ts, cg), F32),
            pltpu.VMEM((nl, ts, LANES), F32),
            pltpu.VMEM((n_runs, cg), F32),
            pltpu.VMEM((n_runs, cg), F32),
            pltpu.VMEM((n_runs, cg), F32),
            pltpu.VMEM((SUBLANES, cg), F32),
        ],
        compiler_params=_params("parallel", "parallel", "arbitrary"),
    )(xr, gate, conv_w.reshape(CONV_WIDTH, c // LANES, 1, LANES),
      conv_b.reshape(c // LANES, 1, LANES), w_gates, b_gates, lam.reshape(1, c))


def _gate_weights(w_r, b_r, w_i, b_i):
    nb, gb, _ = w_r.shape
    groups = nb // GATE_GROUP
    eye = jnp.eye(GATE_GROUP, dtype=w_r.dtype)

    def dense(w):
        w = w.reshape(groups, GATE_GROUP, gb, gb)
        w = jnp.einsum("gjcd,jk->gjckd", w, eye)
        return w.reshape(groups, GATE_GROUP * gb, GATE_GROUP * gb)

    w = 0.5 * jnp.concatenate([dense(w_r), dense(w_i)], axis=2)
    cg = GATE_GROUP * gb
    b = 0.5 * jnp.concatenate([b_r.reshape(groups, 1, cg), b_i.reshape(groups, 1, cg)], axis=2)
    return w.astype(BF16), b.astype(F32)


def _attn_kernel(lam_ref, g_ref, q_ref, k_ref, v_ref, o_ref, acc, *, tq, lambda_init):
    qi = pl.program_id(2)
    hw = 2 * HEAD_DIM
    n_chains = acc.shape[0]
    chains = range(n_chains)
    q = q_ref[0]
    q_parts = [q[:, c * HEAD_DIM:(c + 1) * HEAD_DIM] for c in chains]
    acc[...] = jnp.zeros(acc.shape, F32)

    def step(first_block, n_blocks, carry, masked):
        width = n_blocks * tq
        start = pl.multiple_of(first_block * tq, tq)
        k = k_ref[0, pl.ds(start, width), :]
        v = v_ref[0, pl.ds(start, width), :]
        s = [lax.dot_general(q_parts[c], k[:, c * HEAD_DIM:(c + 1) * HEAD_DIM],
                             (((1,), (1,)), ((), ())), preferred_element_type=F32)
             for c in chains]
        if masked:
            row = lax.broadcasted_iota(jnp.int32, s[0].shape, 0)
            col = lax.broadcasted_iota(jnp.int32, s[0].shape, 1)
            keep = col - (n_blocks - 1) * tq <= row
            s = [jnp.where(keep, s[c], -jnp.inf) for c in chains]
        m_new = [jnp.maximum(carry[2 * c], jnp.max(s[c], axis=-1, keepdims=True)) for c in chains]
        p = [jnp.exp2(s[c] - m_new[c]) for c in chains]
        alpha = [jnp.exp2(carry[2 * c] - m_new[c]) for c in chains]
        l_new = [alpha[c] * carry[2 * c + 1] + jnp.sum(p[c], axis=-1, keepdims=True)
                 for c in chains]
        pv = [jnp.dot(p[c].astype(BF16), v[:, (c // 2) * hw:(c // 2 + 1) * hw],
                      preferred_element_type=F32) for c in chains]
        for c in chains:
            acc[c] = alpha[c] * acc[c] + pv[c]
        out = []
        for c in chains:
            out += [m_new[c], l_new[c]]
        return tuple(out)

    neg = jnp.full((tq, 1), -jnp.inf, F32)
    zero = jnp.zeros((tq, 1), F32)
    n_wide = qi // KV_BLOCKS_PER_STEP
    def wide(block, cr):
        return step(block, KV_BLOCKS_PER_STEP, cr, False)

    carry = lax.fori_loop(
        0, n_wide // 2,
        lambda i, cr: wide((2 * i + 1) * KV_BLOCKS_PER_STEP, wide(2 * i * KV_BLOCKS_PER_STEP, cr)),
        (neg, zero) * n_chains)
    carry = lax.cond(n_wide % 2 == 1,
                     lambda cr: wide((n_wide - 1) * KV_BLOCKS_PER_STEP, cr),
                     lambda cr: cr, carry)
    rest = qi - n_wide * KV_BLOCKS_PER_STEP
    tails = [functools.partial(step, qi - r, r + 1, masked=True)
             for r in range(KV_BLOCKS_PER_STEP)]
    carry = lax.switch(rest, tails, carry)

    lv = lam_ref[0]
    lam = (jnp.exp(jnp.sum(lv[0:1] * lv[1:2], axis=-1, keepdims=True))
           - jnp.exp(jnp.sum(lv[2:3] * lv[3:4], axis=-1, keepdims=True)) + lambda_init)
    for head in range(n_chains // 2):
        c0, c1 = 2 * head, 2 * head + 1
        o = acc[c0] * (1.0 / carry[2 * c0 + 1]) - acc[c1] * (lam / carry[2 * c1 + 1])
        o = o * _rms_scale(o, SUBLN_EPS) * g_ref[...] * (1.0 - lambda_init)
        o_ref[0, :, head * hw:(head + 1) * hw] = o.astype(o_ref.dtype)


def _diff_attention(q, k, v, lam_vecs, subln_g, lambda_init):
    b, l, w = q.shape
    hw = 2 * HEAD_DIM
    bw = HEADS_PER_STEP * hw
    n_chains = 2 * HEADS_PER_STEP
    tq = SEQ_TILE
    kv_spec = pl.BlockSpec((1, l, bw), lambda bi, hi, qi: (bi, 0, hi))
    q_spec = pl.BlockSpec((1, tq, bw), lambda bi, hi, qi: (bi, qi, hi))
    return pl.pallas_call(
        functools.partial(_attn_kernel, tq=tq, lambda_init=lambda_init),
        grid=(b, w // bw, l // tq),
        in_specs=[
            pl.BlockSpec((1, 4, HEAD_DIM), lambda bi, hi, qi: (0, 0, 0)),
            pl.BlockSpec((1, hw), lambda bi, hi, qi: (0, 0)),
            q_spec, kv_spec, kv_spec,
        ],
        out_specs=q_spec,
        out_shape=jax.ShapeDtypeStruct((b, l, w), BF16),
        scratch_shapes=[pltpu.VMEM((n_chains, tq, hw), F32)],
        compiler_params=_params("parallel", "parallel", "arbitrary"),
    )(lam_vecs.reshape(1, 4, HEAD_DIM), subln_g.reshape(1, hw), q, k, v)


def _rope_tables(length, batch):
    inv = 1.0 / (ROPE_THETA ** (jnp.arange(0, HEAD_DIM, 2, dtype=F32) / HEAD_DIM))
    ang = jnp.arange(length, dtype=F32)[:, None] * inv[None, :]
    cos, sin = jnp.cos(ang), jnp.sin(ang)
    cos2 = jnp.concatenate([cos, cos], axis=1)
    sin2 = jnp.concatenate([-sin, sin], axis=1)
    return jnp.tile(cos2, (batch, 1)), jnp.tile(sin2, (batch, 1))


def kernel(x, meta_tokens, a_norm_g, a_w_in, a_conv_w, a_conv_b, a_w_r, a_b_r, a_w_i, a_b_i,
           a_lambda, a_w_out, kv_norm_g, w_kv, b_norm_g, b_w_q, b_lambda, b_subln_g, b_w_o,
           mlp_norm_g, mlp_w1, mlp_w2, final_norm_g):
    batch, seq, d = x.shape
    n_a = a_w_in.shape[0]
    depth = mlp_w1.shape[0]
    d_rnn = a_w_out.shape[1]
    qk_width = b_w_q.shape[2]
    v_width = w_kv.shape[1] - qk_width
    length = N_META_TOKENS + seq
    lp = pl.cdiv(length, SEQ_TILE) * SEQ_TILE
    t = batch * lp
    assert t % TOKEN_TILE == 0

    meta = jnp.broadcast_to(meta_tokens[None].astype(x.dtype), (batch, N_META_TOKENS, d))
    pad = jnp.zeros((batch, lp - length, d), x.dtype)
    h = jnp.concatenate([meta, x, pad], axis=1).reshape(t, d)
    rope = _rope_tables(lp, batch)

    w_in, w_out = a_w_in.astype(BF16), a_w_out.astype(BF16)
    wkv, w_q, w_o = w_kv.astype(BF16)[None], b_w_q.astype(BF16), b_w_o.astype(BF16)
    q_scale = math.log2(math.e) / math.sqrt(HEAD_DIM)

    k_sh = v_sh = None
    for layer in range(depth):
        if layer < n_a:
            j = layer
            gate = _norm_matmul(h, a_norm_g[j], w_in, j, 0, d_rnn, BF16, act="gelu")
            xr = _norm_matmul(h, a_norm_g[j], w_in, j, d_rnn, d_rnn, F32)
            w_g, b_g = _gate_weights(a_w_r[j], a_b_r[j], a_w_i[j], a_b_i[j])
            y = _rglru(xr.reshape(batch, lp, d_rnn), gate.reshape(batch, lp, d_rnn),
                       a_conv_w[j], a_conv_b[j], w_g, b_g, a_lambda[j])
            h = _matmul_residual(y.reshape(t, d_rnn), w_out, j, h)
        else:
            j = layer - n_a
            if j == 0:
                k_sh = _norm_matmul(h, kv_norm_g, wkv, 0, 0, qk_width, BF16, rope=rope)
                v_sh = _norm_matmul(h, kv_norm_g, wkv, 0, qk_width, v_width, BF16)
                k_sh = k_sh.reshape(batch, lp, qk_width)
                v_sh = v_sh.reshape(batch, lp, v_width)
            lambda_init = 0.8 - 0.6 * math.exp(-0.3 * layer)
            q = _norm_matmul(h, b_norm_g[j], w_q, j, 0, qk_width, BF16, rope=rope, scale=q_scale)
            o = _diff_attention(q.reshape(batch, lp, qk_width), k_sh, v_sh, b_lambda[j],
                                b_subln_g[j], lambda_init)
            h = _matmul_residual(o.reshape(t, v_width), w_o, j, h)
        final_g = final_norm_g if layer == depth - 1 else None
        h = _mlp(h, mlp_norm_g[layer], mlp_w1, mlp_w2, layer, final_g)
    return h.reshape(batch, lp, d)[:, N_META_TOKENS:length]
```

```python
import functools
import math

import jax
import jax.numpy as jnp
from jax import lax
from jax.experimental import pallas as pl
from jax.experimental.pallas import tpu as pltpu

F32 = jnp.float32
BF16 = jnp.bfloat16

N_META_TOKENS = 16
CONV_WIDTH = 4
LRU_C = 8.0
HEAD_DIM = 128
ROPE_THETA = 10000.0
NORM_EPS = 1e-6
SUBLN_EPS = 1e-5

LANES = 128
SUBLANES = 8
VMEM_LIMIT_BYTES = 56 * 1024 * 1024

SEQ_TILE = 384
TOKEN_TILE = 768
FF_TILE = 512
GATE_GROUP = 4
SCAN_ROWS = 264
RGLRU_TILE = 528
KV_BLOCKS_PER_STEP = 2
HEADS_PER_STEP = 2

def _params(*semantics):
    return pltpu.CompilerParams(dimension_semantics=semantics,
                                vmem_limit_bytes=VMEM_LIMIT_BYTES)


def _rms_scale(x, eps):
    return lax.rsqrt(jnp.mean(x * x, axis=-1, keepdims=True) + eps)


def _norm_matmul_kernel(*refs, act, rope, scale):
    if rope:
        x_ref, g_ref, w_ref, cos_ref, sin_ref, o_ref = refs
    else:
        x_ref, g_ref, w_ref, o_ref = refs

    x = x_ref[...]
    xn = (x * _rms_scale(x, NORM_EPS) * g_ref[...]).astype(BF16)
    y = jnp.dot(xn, w_ref[...], preferred_element_type=F32)
    if act == "gelu":
        y = jax.nn.gelu(y)
    if rope:
        c = cos_ref[...]
        s = sin_ref[...]
        segs = []
        for i in range(y.shape[1] // HEAD_DIM):
            seg = y[:, i * HEAD_DIM:(i + 1) * HEAD_DIM]
            segs.append(seg * c + pltpu.roll(seg, HEAD_DIM // 2, axis=1) * s)
        y = jnp.concatenate(segs, axis=1)
    if scale != 1.0:
        y = y * scale
    o_ref[...] = y.astype(o_ref.dtype)


def _norm_matmul(x, g, w, layer, col0, n, out_dtype, act=None, rope=None, scale=1.0):
    t, d = x.shape
    tm = TOKEN_TILE
    assert col0 % n == 0
    j0 = col0 // n
    in_specs = [
        pl.BlockSpec((tm, d), lambda i: (i, 0)),
        pl.BlockSpec((1, d), lambda i: (0, 0)),
        pl.BlockSpec((None, d, n), lambda i: (layer, 0, j0), pipeline_mode=pl.Buffered(1)),
    ]
    args = [x, g.reshape(1, d), w]
    if rope is not None:
        in_specs += [pl.BlockSpec((tm, HEAD_DIM), lambda i: (i, 0))] * 2
        args += list(rope)
    return pl.pallas_call(
        functools.partial(_norm_matmul_kernel, act=act, rope=rope is not None, scale=scale),
        grid=(t // tm,),
        in_specs=in_specs,
        out_specs=pl.BlockSpec((tm, n), lambda i: (i, 0)),
        out_shape=jax.ShapeDtypeStruct((t, n), out_dtype),
        compiler_params=_params("parallel"),
    )(*args)


def _matmul_residual_kernel(y_ref, w_ref, r_ref, o_ref):
    o_ref[...] = r_ref[...] + jnp.dot(y_ref[...], w_ref[...], preferred_element_type=F32)


def _matmul_residual(y, w, layer, res):
    t, k = y.shape
    n = w.shape[2]
    tm = TOKEN_TILE
    return pl.pallas_call(
        _matmul_residual_kernel,
        grid=(t // tm,),
        in_specs=[
            pl.BlockSpec((tm, k), lambda i: (i, 0)),
            pl.BlockSpec((None, k, n), lambda i: (layer, 0, 0), pipeline_mode=pl.Buffered(1)),
            pl.BlockSpec((tm, n), lambda i: (i, 0)),
        ],
        out_specs=pl.BlockSpec((tm, n), lambda i: (i, 0)),
        out_shape=jax.ShapeDtypeStruct((t, n), F32),
        compiler_params=_params("parallel"),
    )(y, w, res)


def _mlp_kernel(*refs, final):
    if final:
        x_ref, g_ref, w1_ref, w2_ref, fg_ref, o_ref, xn_ref = refs
    else:
        x_ref, g_ref, w1_ref, w2_ref, o_ref, xn_ref = refs
    j = pl.program_id(1)

    @pl.when(j == 0)
    def _():
        x = x_ref[...]
        xn_ref[...] = (x * _rms_scale(x, NORM_EPS) * g_ref[...]).astype(xn_ref.dtype)
        o_ref[...] = x

    h = jnp.dot(xn_ref[...], w1_ref[...].astype(BF16), preferred_element_type=F32)
    h = jnp.square(jnp.maximum(h, 0.0)).astype(BF16)
    o_ref[...] += jnp.dot(h, w2_ref[...].astype(BF16), preferred_element_type=F32)

    if final:
        @pl.when(j == pl.num_programs(1) - 1)
        def _():
            y = o_ref[...]
            o_ref[...] = y * _rms_scale(y, NORM_EPS) * fg_ref[...]


def _mlp(x, g, w1, w2, layer, final_g=None):
    t, d = x.shape
    f = w1.shape[2]
    tm, tf = TOKEN_TILE, FF_TILE
    in_specs = [
        pl.BlockSpec((tm, d), lambda i, j: (i, 0)),
        pl.BlockSpec((1, d), lambda i, j: (0, 0)),
        pl.BlockSpec((None, d, tf), lambda i, j: (layer, 0, j)),
        pl.BlockSpec((None, tf, d), lambda i, j: (layer, j, 0)),
    ]
    args = [x, g.reshape(1, d), w1, w2]
    if final_g is not None:
        in_specs.append(pl.BlockSpec((1, d), lambda i, j: (0, 0)))
        args.append(final_g.reshape(1, d))
    return pl.pallas_call(
        functools.partial(_mlp_kernel, final=final_g is not None),
        grid=(t // tm, f // tf),
        in_specs=in_specs,
        out_specs=pl.BlockSpec((tm, d), lambda i, j: (i, 0)),
        out_shape=jax.ShapeDtypeStruct((t, d), F32),
        scratch_shapes=[pltpu.VMEM((tm, d), BF16)],
        compiler_params=_params("parallel", "arbitrary"),
    )(*args)


def _rglru_kernel(xr_ref, gate_ref, cw_ref, cb_ref, w_ref, b_ref, lam_ref, y_ref,
                  xl, xc_scr, hl, aend, bend, cin, hcar, *, ts):
    c = xr_ref.shape[2]
    n_lane_tiles = c // LANES
    lane_tiles = range(n_lane_tiles)
    seg = SCAN_ROWS // SUBLANES
    n_groups = ts // SCAN_ROWS
    hdr = SUBLANES

    @pl.when(pl.program_id(2) == 0)
    def _():
        xl[:, 0:hdr, :] = jnp.zeros((n_lane_tiles, hdr, LANES), F32)
        hcar[...] = jnp.zeros(hcar.shape, F32)

    for j in lane_tiles:
        xl[j, hdr:hdr + ts, :] = xr_ref[0, :, j * LANES:(j + 1) * LANES]

    def lane(x, j):
        return x[:, j * LANES:(j + 1) * LANES]

    taps = [cw_ref[tap] for tap in range(CONV_WIDTH)]
    bias = cb_ref[...]
    for g in range(n_groups):
        loads = {}
        for k in range(-(CONV_WIDTH - 1), seg):
            loads[k] = xl[:, pl.ds(hdr + g * SCAN_ROWS + k, SUBLANES, stride=seg), :]
        for k in range(seg):
            xc = loads[k] * taps[CONV_WIDTH - 1] + bias
            for back in range(1, CONV_WIDTH):
                xc = xc + loads[k - back] * taps[CONV_WIDTH - 1 - back]
            r0 = g * SCAN_ROWS + k * SUBLANES
            for j in lane_tiles:
                xc_scr[r0:r0 + SUBLANES, j * LANES:(j + 1) * LANES] = xc[j]
    xl[:, 0:hdr, :] = xl[:, ts:ts + hdr, :]

    xc = xc_scr[...]
    half_gates = jnp.dot(xc.astype(BF16), w_ref[0], preferred_element_type=F32) + b_ref[0]
    t_r = jnp.tanh(half_gates[:, :c])
    t_i = jnp.tanh(half_gates[:, c:])
    half_rate = (-0.5 * LRU_C * math.log2(math.e)) * jax.nn.softplus(-lam_ref[...])
    a = jnp.exp2(t_r * half_rate + half_rate)
    bv = jnp.exp(0.5 * jnp.log(1.0 - a * a)) * ((0.5 * t_i + 0.5) * xc)

    def rows(x, g, k):
        r0 = g * SCAN_ROWS + k * SUBLANES
        return x[r0:r0 + SUBLANES, :]

    for g in range(n_groups):
        a_cum, b_cum = rows(a, g, 0), rows(bv, g, 0)
        for k in range(1, seg):
            a_k = rows(a, g, k)
            b_cum = a_k * b_cum + rows(bv, g, k)
            a_cum = a_k * a_cum
        aend[g * SUBLANES:(g + 1) * SUBLANES, :] = a_cum
        bend[g * SUBLANES:(g + 1) * SUBLANES, :] = b_cum

    carry = hcar[0:1, :]
    for s in range(n_groups * SUBLANES):
        cin[s:s + 1, :] = carry
        carry = aend[s:s + 1, :] * carry + bend[s:s + 1, :]
    hcar[0:1, :] = carry

    for g in range(n_groups):
        h = cin[g * SUBLANES:(g + 1) * SUBLANES, :]
        for k in range(seg):
            h = rows(a, g, k) * h + rows(bv, g, k)
            for j in lane_tiles:
                hl[j, pl.ds(g * SCAN_ROWS + k, SUBLANES, stride=seg), :] = lane(h, j)
    for j in lane_tiles:
        lanes = slice(j * LANES, (j + 1) * LANES)
        y_ref[0, :, lanes] = (hl[j] * gate_ref[0, :, lanes].astype(F32)).astype(y_ref.dtype)


def _rglru(xr, gate, conv_w, conv_b, w_gates, b_gates, lam):
    b, l, c = xr.shape
    groups = w_gates.shape[0]
    cg = c // groups
    nl = cg // LANES
    ts = RGLRU_TILE
    assert l % ts == 0
    n_runs = ts // SCAN_ROWS * SUBLANES
    seq_spec = pl.BlockSpec((1, ts, cg), lambda bi, gi, ci: (bi, ci, gi))
    return pl.pallas_call(
        functools.partial(_rglru_kernel, ts=ts),
        grid=(b, groups, l // ts),
        in_specs=[
            seq_spec, seq_spec,
            pl.BlockSpec((CONV_WIDTH, nl, 1, LANES), lambda bi, gi, ci: (0, gi, 0, 0)),
            pl.BlockSpec((nl, 1, LANES), lambda bi, gi, ci: (gi, 0, 0)),
            pl.BlockSpec((1, cg, 2 * cg), lambda bi, gi, ci: (gi, 0, 0)),
            pl.BlockSpec((1, 1, 2 * cg), lambda bi, gi, ci: (gi, 0, 0)),
            pl.BlockSpec((1, cg), lambda bi, gi, ci: (0, gi)),
        ],
        out_specs=seq_spec,
        out_shape=jax.ShapeDtypeStruct((b, l, c), BF16),
        scratch_shapes=[
            pltpu.VMEM((nl, SUBLANES + ts, LANES), F32),
            pltpu.VMEM((ts, cg), F32),
            pltpu.VMEM((nl, ts, LANES), F32),
            pltpu.VMEM((n_runs, cg), F32),
            pltpu.VMEM((n_runs, cg), F32),
            pltpu.VMEM((n_runs, cg), F32),
            pltpu.VMEM((SUBLANES, cg), F32),
        ],
        compiler_params=_params("parallel", "parallel", "arbitrary"),
    )(xr, gate, conv_w.reshape(CONV_WIDTH, c // LANES, 1, LANES),
      conv_b.reshape(c // LANES, 1, LANES), w_gates, b_gates, lam.reshape(1, c))


def _gate_weights(w_r, b_r, w_i, b_i):
    nb, gb, _ = w_r.shape
    groups = nb // GATE_GROUP
    def dense(w):
        w = w.reshape(groups, GATE_GROUP, gb, gb)
        cols = [jnp.where((jnp.arange(GATE_GROUP) == k)[None, :, None, None], w, 0.0)
                .reshape(groups, GATE_GROUP * gb, gb) for k in range(GATE_GROUP)]
        return jnp.concatenate(cols, axis=2)

    w = 0.5 * jnp.concatenate([dense(w_r), dense(w_i)], axis=2)
    cg = GATE_GROUP * gb
    b = 0.5 * jnp.concatenate([b_r.reshape(groups, 1, cg), b_i.reshape(groups, 1, cg)], axis=2)
    return w.astype(BF16), b.astype(F32)


def _attn_kernel(lam_ref, g_ref, q_ref, k_ref, v_ref, o_ref, acc, *, tq, lambda_init):
    qi = pl.program_id(2)
    hw = 2 * HEAD_DIM
    n_chains = acc.shape[0]
    chains = range(n_chains)
    q = q_ref[0]
    q_parts = [q[:, c * HEAD_DIM:(c + 1) * HEAD_DIM] for c in chains]
    acc[...] = jnp.zeros(acc.shape, F32)

    def step(first_block, n_blocks, carry, masked):
        width = n_blocks * tq
        start = pl.multiple_of(first_block * tq, tq)
        k = k_ref[0, pl.ds(start, width), :]
        v = v_ref[0, pl.ds(start, width), :]
        s = [lax.dot_general(q_parts[c], k[:, c * HEAD_DIM:(c + 1) * HEAD_DIM],
                             (((1,), (1,)), ((), ())), preferred_element_type=F32)
             for c in chains]
        if masked:
            row = lax.broadcasted_iota(jnp.int32, s[0].shape, 0)
            col = lax.broadcasted_iota(jnp.int32, s[0].shape, 1)
            keep = col - (n_blocks - 1) * tq <= row
            s = [jnp.where(keep, s[c], -jnp.inf) for c in chains]
        m_new = [jnp.maximum(carry[2 * c], jnp.max(s[c], axis=-1, keepdims=True)) for c in chains]
        p = [jnp.exp2(s[c] - m_new[c]) for c in chains]
        alpha = [jnp.exp2(carry[2 * c] - m_new[c]) for c in chains]
        l_new = [alpha[c] * carry[2 * c + 1] + jnp.sum(p[c], axis=-1, keepdims=True)
                 for c in chains]
        pv = [jnp.dot(p[c].astype(BF16), v[:, (c // 2) * hw:(c // 2 + 1) * hw],
                      preferred_element_type=F32) for c in chains]
        for c in chains:
            acc[c] = alpha[c] * acc[c] + pv[c]
        out = []
        for c in chains:
            out += [m_new[c], l_new[c]]
        return tuple(out)

    neg = jnp.full((tq, 1), -jnp.inf, F32)
    zero = jnp.zeros((tq, 1), F32)
    n_wide = qi // KV_BLOCKS_PER_STEP
    def wide(block, cr):
        return step(block, KV_BLOCKS_PER_STEP, cr, False)

    carry = lax.fori_loop(
        0, n_wide // 2,
        lambda i, cr: wide((2 * i + 1) * KV_BLOCKS_PER_STEP, wide(2 * i * KV_BLOCKS_PER_STEP, cr)),
        (neg, zero) * n_chains)
    carry = lax.cond(n_wide % 2 == 1,
                     lambda cr: wide((n_wide - 1) * KV_BLOCKS_PER_STEP, cr),
                     lambda cr: cr, carry)
    rest = qi - n_wide * KV_BLOCKS_PER_STEP
    tails = [functools.partial(step, qi - r, r + 1, masked=True)
             for r in range(KV_BLOCKS_PER_STEP)]
    carry = lax.switch(rest, tails, carry)

    lv = lam_ref[0]
    lam = (jnp.exp(jnp.sum(lv[0:1] * lv[1:2], axis=-1, keepdims=True))
           - jnp.exp(jnp.sum(lv[2:3] * lv[3:4], axis=-1, keepdims=True)) + lambda_init)
    for head in range(n_chains // 2):
        c0, c1 = 2 * head, 2 * head + 1
        o = acc[c0] * (1.0 / carry[2 * c0 + 1]) - acc[c1] * (lam / carry[2 * c1 + 1])
        o = o * _rms_scale(o, SUBLN_EPS) * g_ref[...] * (1.0 - lambda_init)
        o_ref[0, :, head * hw:(head + 1) * hw] = o.astype(o_ref.dtype)


def _diff_attention(q, k, v, lam_vecs, subln_g, lambda_init):
    b, l, w = q.shape
    hw = 2 * HEAD_DIM
    bw = HEADS_PER_STEP * hw
    n_chains = 2 * HEADS_PER_STEP
    tq = SEQ_TILE
    kv_spec = pl.BlockSpec((1, l, bw), lambda bi, hi, qi: (bi, 0, hi))
    q_spec = pl.BlockSpec((1, tq, bw), lambda bi, hi, qi: (bi, qi, hi))
    return pl.pallas_call(
        functools.partial(_attn_kernel, tq=tq, lambda_init=lambda_init),
        grid=(b, w // bw, l // tq),
        in_specs=[
            pl.BlockSpec((1, 4, HEAD_DIM), lambda bi, hi, qi: (0, 0, 0)),
            pl.BlockSpec((1, hw), lambda bi, hi, qi: (0, 0)),
            q_spec, kv_spec, kv_spec,
        ],
        out_specs=q_spec,
        out_shape=jax.ShapeDtypeStruct((b, l, w), BF16),
        scratch_shapes=[pltpu.VMEM((n_chains, tq, hw), F32)],
        compiler_params=_params("parallel", "parallel", "arbitrary"),
    )(lam_vecs.reshape(1, 4, HEAD_DIM), subln_g.reshape(1, hw), q, k, v)


def _rope_tables(length, batch):
    inv = 1.0 / (ROPE_THETA ** (jnp.arange(0, HEAD_DIM, 2, dtype=F32) / HEAD_DIM))
    ang = jnp.arange(length, dtype=F32)[:, None] * inv[None, :]
    cos, sin = jnp.cos(ang), jnp.sin(ang)
    cos2 = jnp.concatenate([cos, cos], axis=1)
    sin2 = jnp.concatenate([-sin, sin], axis=1)
    return jnp.tile(cos2, (batch, 1)), jnp.tile(sin2, (batch, 1))


def kernel(x, meta_tokens, a_norm_g, a_w_in, a_conv_w, a_conv_b, a_w_r, a_b_r, a_w_i, a_b_i,
           a_lambda, a_w_out, kv_norm_g, w_kv, b_norm_g, b_w_q, b_lambda, b_subln_g, b_w_o,
           mlp_norm_g, mlp_w1, mlp_w2, final_norm_g):
    batch, seq, d = x.shape
    n_a = a_w_in.shape[0]
    depth = mlp_w1.shape[0]
    d_rnn = a_w_out.shape[1]
    qk_width = b_w_q.shape[2]
    v_width = w_kv.shape[1] - qk_width
    length = N_META_TOKENS + seq
    lp = pl.cdiv(length, SEQ_TILE) * SEQ_TILE
    t = batch * lp
    assert t % TOKEN_TILE == 0

    meta = jnp.broadcast_to(meta_tokens[None].astype(x.dtype), (batch, N_META_TOKENS, d))
    pad = jnp.zeros((batch, lp - length, d), x.dtype)
    h = jnp.concatenate([meta, x, pad], axis=1).reshape(t, d)
    rope = _rope_tables(lp, batch)

    w_in, w_out = a_w_in.astype(BF16), a_w_out.astype(BF16)
    wkv, w_q, w_o = w_kv.astype(BF16)[None], b_w_q.astype(BF16), b_w_o.astype(BF16)
    q_scale = math.log2(math.e) / math.sqrt(HEAD_DIM)

    k_sh = v_sh = None
    for layer in range(depth):
        if layer < n_a:
            j = layer
            gate = _norm_matmul(h, a_norm_g[j], w_in, j, 0, d_rnn, BF16, act="gelu")
            xr = _norm_matmul(h, a_norm_g[j], w_in, j, d_rnn, d_rnn, F32)
            w_g, b_g = _gate_weights(a_w_r[j], a_b_r[j], a_w_i[j], a_b_i[j])
            y = _rglru(xr.reshape(batch, lp, d_rnn), gate.reshape(batch, lp, d_rnn),
                       a_conv_w[j], a_conv_b[j], w_g, b_g, a_lambda[j])
            h = _matmul_residual(y.reshape(t, d_rnn), w_out, j, h)
        else:
            j = layer - n_a
            if j == 0:
                k_sh = _norm_matmul(h, kv_norm_g, wkv, 0, 0, qk_width, BF16, rope=rope)
                v_sh = _norm_matmul(h, kv_norm_g, wkv, 0, qk_width, v_width, BF16)
                k_sh = k_sh.reshape(batch, lp, qk_width)
                v_sh = v_sh.reshape(batch, lp, v_width)
            lambda_init = 0.8 - 0.6 * math.exp(-0.3 * layer)
            q = _norm_matmul(h, b_norm_g[j], w_q, j, 0, qk_width, BF16, rope=rope, scale=q_scale)
            o = _diff_attention(q.reshape(batch, lp, qk_width), k_sh, v_sh, b_lambda[j],
                                b_subln_g[j], lambda_init)
            h = _matmul_residual(o.reshape(t, v_width), w_o, j, h)
        final_g = final_norm_g if layer == depth - 1 else None
        h = _mlp(h, mlp_norm_g[layer], mlp_w1, mlp_w2, layer, final_g)
    return h.reshape(batch, lp, d)[:, N_META_TOKENS:length]
```

```python
import functools
import math

import jax
import jax.numpy as jnp
from jax import lax
from jax.experimental import pallas as pl
from jax.experimental.pallas import tpu as pltpu

F32 = jnp.float32
BF16 = jnp.bfloat16

N_META_TOKENS = 16
CONV_WIDTH = 4
LRU_C = 8.0
HEAD_DIM = 128
ROPE_THETA = 10000.0
NORM_EPS = 1e-6
SUBLN_EPS = 1e-5

LANES = 128
SUBLANES = 8
VMEM_LIMIT_BYTES = 56 * 1024 * 1024

SEQ_TILE = 384
TOKEN_TILE = 768
FF_TILE = 512
GATE_GROUP = 4
SCAN_ROWS = 264
RGLRU_TILE = 528
KV_BLOCKS_PER_STEP = 2
HEADS_PER_STEP = 2

def _params(*semantics):
    return pltpu.CompilerParams(dimension_semantics=semantics,
                                vmem_limit_bytes=VMEM_LIMIT_BYTES)


def _rms_scale(x, eps):
    return lax.rsqrt(jnp.mean(x * x, axis=-1, keepdims=True) + eps)


def _norm_matmul_kernel(*refs, act, rope, scale):
    if rope:
        x_ref, g_ref, w_ref, cos_ref, sin_ref, o_ref = refs
    else:
        x_ref, g_ref, w_ref, o_ref = refs

    x = x_ref[...]
    xn = (x * _rms_scale(x, NORM_EPS) * g_ref[...]).astype(BF16)
    y = jnp.dot(xn, w_ref[...], preferred_element_type=F32)
    if act == "gelu":
        y = jax.nn.gelu(y)
    if rope:
        c = cos_ref[...]
        s = sin_ref[...]
        segs = []
        for i in range(y.shape[1] // HEAD_DIM):
            seg = y[:, i * HEAD_DIM:(i + 1) * HEAD_DIM]
            segs.append(seg * c + pltpu.roll(seg, HEAD_DIM // 2, axis=1) * s)
        y = jnp.concatenate(segs, axis=1)
    if scale != 1.0:
        y = y * scale
    o_ref[...] = y.astype(o_ref.dtype)


def _norm_matmul(x, g, w, layer, col0, n, out_dtype, act=None, rope=None, scale=1.0):
    t, d = x.shape
    tm = TOKEN_TILE
    assert col0 % n == 0
    j0 = col0 // n
    in_specs = [
        pl.BlockSpec((tm, d), lambda i: (i, 0)),
        pl.BlockSpec((1, d), lambda i: (0, 0)),
        pl.BlockSpec((None, d, n), lambda i: (layer, 0, j0), pipeline_mode=pl.Buffered(1)),
    ]
    args = [x, g.reshape(1, d), w]
    if rope is not None:
        in_specs += [pl.BlockSpec((tm, HEAD_DIM), lambda i: (i, 0))] * 2
        args += list(rope)
    return pl.pallas_call(
        functools.partial(_norm_matmul_kernel, act=act, rope=rope is not None, scale=scale),
        grid=(t // tm,),
        in_specs=in_specs,
        out_specs=pl.BlockSpec((tm, n), lambda i: (i, 0)),
        out_shape=jax.ShapeDtypeStruct((t, n), out_dtype),
        compiler_params=_params("parallel"),
    )(*args)


def _matmul_residual_kernel(y_ref, w_ref, r_ref, o_ref):
    o_ref[...] = r_ref[...] + jnp.dot(y_ref[...], w_ref[...], preferred_element_type=F32)


def _matmul_residual(y, w, layer, res):
    t, k = y.shape
    n = w.shape[2]
    tm = TOKEN_TILE
    return pl.pallas_call(
        _matmul_residual_kernel,
        grid=(t // tm,),
        in_specs=[
            pl.BlockSpec((tm, k), lambda i: (i, 0)),
            pl.BlockSpec((None, k, n), lambda i: (layer, 0, 0), pipeline_mode=pl.Buffered(1)),
            pl.BlockSpec((tm, n), lambda i: (i, 0)),
        ],
        out_specs=pl.BlockSpec((tm, n), lambda i: (i, 0)),
        out_shape=jax.ShapeDtypeStruct((t, n), F32),
        compiler_params=_params("parallel"),
    )(y, w, res)


def _mlp_kernel(*refs, segments):
    final = segments is not None
    if final:
        x_ref, g_ref, w1_ref, w2_ref, fg_ref, o_ref, out_hbm, xn_ref, stage, sems = refs
    else:
        x_ref, g_ref, w1_ref, w2_ref, o_ref, xn_ref = refs
    i = pl.program_id(0)
    j = pl.program_id(1)

    @pl.when(j == 0)
    def _():
        x = x_ref[...]
        xn_ref[...] = (x * _rms_scale(x, NORM_EPS) * g_ref[...]).astype(xn_ref.dtype)
        o_ref[...] = x

    h = jnp.dot(xn_ref[...], w1_ref[...].astype(BF16), preferred_element_type=F32)
    h = jnp.square(jnp.maximum(h, 0.0)).astype(BF16)
    o_ref[...] += jnp.dot(h, w2_ref[...].astype(BF16), preferred_element_type=F32)

    if final:
        def copies(tile):
            return [pltpu.make_async_copy(stage.at[pl.ds(row, count)],
                                          out_hbm.at[pl.ds(dst, count)], sems.at[n])
                    for n, (row, count, dst) in enumerate(segments[tile])]

        def per_tile(action):
            for tile in range(len(segments)):
                pl.when(i == tile)(functools.partial(action, tile))

        def wait_previous(tile):
            if tile > 0:
                for cp in copies(tile - 1):
                    cp.wait()

        def start_current(tile):
            for cp in copies(tile):
                cp.start()
            if tile == len(segments) - 1:
                for cp in copies(tile):
                    cp.wait()

        @pl.when(j == pl.num_programs(1) - 1)
        def _():
            per_tile(wait_previous)
            y = o_ref[...]
            stage[...] = y * _rms_scale(y, NORM_EPS) * fg_ref[...]
            per_tile(start_current)


def _output_segments(n_tiles, tm, lp, length, seq):
    table = []
    for tile in range(n_tiles):
        first = tile * tm
        pieces = []
        for b in range(first // lp, (first + tm - 1) // lp + 1):
            lo = max(first, b * lp + N_META_TOKENS)
            hi = min(first + tm, b * lp + length)
            if hi > lo:
                pieces.append((lo - first, hi - lo, b * seq + lo - b * lp - N_META_TOKENS))
        table.append(tuple(pieces))
    return tuple(table)


def _mlp(x, g, w1, w2, layer, final=None):
    t, d = x.shape
    f = w1.shape[2]
    tm, tf = TOKEN_TILE, FF_TILE
    in_specs = [
        pl.BlockSpec((tm, d), lambda i, j: (i, 0)),
        pl.BlockSpec((1, d), lambda i, j: (0, 0)),
        pl.BlockSpec((None, d, tf), lambda i, j: (layer, 0, j)),
        pl.BlockSpec((None, tf, d), lambda i, j: (layer, j, 0)),
    ]
    args = [x, g.reshape(1, d), w1, w2]
    out_specs = pl.BlockSpec((tm, d), lambda i, j: (i, 0))
    out_shape = jax.ShapeDtypeStruct((t, d), F32)
    scratch_shapes = [pltpu.VMEM((tm, d), BF16)]
    if final is None:
        return pl.pallas_call(
            functools.partial(_mlp_kernel, segments=None),
            grid=(t // tm, f // tf),
            in_specs=in_specs,
            out_specs=out_specs,
            out_shape=out_shape,
            scratch_shapes=scratch_shapes,
            compiler_params=_params("parallel", "arbitrary"),
        )(*args)
    final_g, batch, lp, length = final
    seq = length - N_META_TOKENS
    segments = _output_segments(t // tm, tm, lp, length, seq)
    assert all(row % SUBLANES == 0 and count % SUBLANES == 0 and dst % SUBLANES == 0
               for pieces in segments for row, count, dst in pieces)
    max_pieces = max(len(pieces) for pieces in segments)
    _, out = pl.pallas_call(
        functools.partial(_mlp_kernel, segments=segments),
        grid=(t // tm, f // tf),
        in_specs=in_specs + [pl.BlockSpec((1, d), lambda i, j: (0, 0))],
        out_specs=[out_specs, pl.BlockSpec(memory_space=pl.ANY)],
        out_shape=[out_shape, jax.ShapeDtypeStruct((batch * seq, d), F32)],
        scratch_shapes=scratch_shapes + [pltpu.VMEM((tm, d), F32),
                                         pltpu.SemaphoreType.DMA((max_pieces,))],
        compiler_params=_params("arbitrary", "arbitrary"),
    )(*args, final_g.reshape(1, d))
    return out


def _rglru_kernel(xr_ref, gate_ref, cw_ref, cb_ref, w_ref, b_ref, lam_ref, y_ref,
                  xl, xc_scr, hl, aend, bend, cin, hcar, *, ts):
    c = xr_ref.shape[2]
    n_lane_tiles = c // LANES
    lane_tiles = range(n_lane_tiles)
    seg = SCAN_ROWS // SUBLANES
    n_groups = ts // SCAN_ROWS
    hdr = SUBLANES

    @pl.when(pl.program_id(2) == 0)
    def _():
        xl[:, 0:hdr, :] = jnp.zeros((n_lane_tiles, hdr, LANES), F32)
        hcar[...] = jnp.zeros(hcar.shape, F32)

    for j in lane_tiles:
        xl[j, hdr:hdr + ts, :] = xr_ref[0, :, j * LANES:(j + 1) * LANES]

    def lane(x, j):
        return x[:, j * LANES:(j + 1) * LANES]

    taps = [cw_ref[tap] for tap in range(CONV_WIDTH)]
    bias = cb_ref[...]
    for g in range(n_groups):
        loads = {}
        for k in range(-(CONV_WIDTH - 1), seg):
            loads[k] = xl[:, pl.ds(hdr + g * SCAN_ROWS + k, SUBLANES, stride=seg), :]
        for k in range(seg):
            xc = loads[k] * taps[CONV_WIDTH - 1] + bias
            for back in range(1, CONV_WIDTH):
                xc = xc + loads[k - back] * taps[CONV_WIDTH - 1 - back]
            r0 = g * SCAN_ROWS + k * SUBLANES
            for j in lane_tiles:
                xc_scr[r0:r0 + SUBLANES, j * LANES:(j + 1) * LANES] = xc[j]
    xl[:, 0:hdr, :] = xl[:, ts:ts + hdr, :]

    xc = xc_scr[...]
    half_gates = jnp.dot(xc.astype(BF16), w_ref[0], preferred_element_type=F32) + b_ref[0]
    t_r = jnp.tanh(half_gates[:, :c])
    t_i = jnp.tanh(half_gates[:, c:])
    half_rate = (-0.5 * LRU_C * math.log2(math.e)) * jax.nn.softplus(-lam_ref[...])
    a = jnp.exp2(t_r * half_rate + half_rate)
    bv = jnp.exp(0.5 * jnp.log(1.0 - a * a)) * ((0.5 * t_i + 0.5) * xc)

    def rows(x, g, k):
        r0 = g * SCAN_ROWS + k * SUBLANES
        return x[r0:r0 + SUBLANES, :]

    for g in range(n_groups):
        a_cum, b_cum = rows(a, g, 0), rows(bv, g, 0)
        for k in range(1, seg):
            a_k = rows(a, g, k)
            b_cum = a_k * b_cum + rows(bv, g, k)
            a_cum = a_k * a_cum
        aend[g * SUBLANES:(g + 1) * SUBLANES, :] = a_cum
        bend[g * SUBLANES:(g + 1) * SUBLANES, :] = b_cum

    carry = hcar[0:1, :]
    for s in range(n_groups * SUBLANES):
        cin[s:s + 1, :] = carry
        carry = aend[s:s + 1, :] * carry + bend[s:s + 1, :]
    hcar[0:1, :] = carry

    for g in range(n_groups):
        h = cin[g * SUBLANES:(g + 1) * SUBLANES, :]
        for k in range(seg):
            h = rows(a, g, k) * h + rows(bv, g, k)
            for j in lane_tiles:
                hl[j, pl.ds(g * SCAN_ROWS + k, SUBLANES, stride=seg), :] = lane(h, j)
    for j in lane_tiles:
        lanes = slice(j * LANES, (j + 1) * LANES)
        y_ref[0, :, lanes] = (hl[j] * gate_ref[0, :, lanes].astype(F32)).astype(y_ref.dtype)


def _rglru(xr, gate, conv_w, conv_b, w_gates, b_gates, lam):
    b, l, c = xr.shape
    groups = w_gates.shape[0]
    cg = c // groups
    nl = cg // LANES
    ts = RGLRU_TILE
    assert l % ts == 0
    n_runs = ts // SCAN_ROWS * SUBLANES
    seq_spec = pl.BlockSpec((1, ts, cg), lambda bi, gi, ci: (bi, ci, gi))
    return pl.pallas_call(
        functools.partial(_rglru_kernel, ts=ts),
        grid=(b, groups, l // ts),
        in_specs=[
            seq_spec, seq_spec,
            pl.BlockSpec((CONV_WIDTH, nl, 1, LANES), lambda bi, gi, ci: (0, gi, 0, 0)),
            pl.BlockSpec((nl, 1, LANES), lambda bi, gi, ci: (gi, 0, 0)),
            pl.BlockSpec((1, cg, 2 * cg), lambda bi, gi, ci: (gi, 0, 0)),
            pl.BlockSpec((1, 1, 2 * cg), lambda bi, gi, ci: (gi, 0, 0)),
            pl.BlockSpec((1, cg), lambda bi, gi, ci: (0, gi)),
        ],
        out_specs=seq_spec,
        out_shape=jax.ShapeDtypeStruct((b, l, c), BF16),
        scratch_shapes=[
            pltpu.VMEM((nl, SUBLANES + ts, LANES), F32),
            pltpu.VMEM((ts, cg), F32),
            pltpu.VMEM((nl, ts, LANES), F32),
            pltpu.VMEM((n_runs, cg), F32),
            pltpu.VMEM((n_runs, cg), F32),
            pltpu.VMEM((n_runs, cg), F32),
            pltpu.VMEM((SUBLANES, cg), F32),
        ],
        compiler_params=_params("parallel", "parallel", "arbitrary"),
    )(xr, gate, conv_w.reshape(CONV_WIDTH, c // LANES, 1, LANES),
      conv_b.reshape(c // LANES, 1, LANES), w_gates, b_gates, lam.reshape(1, c))


def _gate_weights(w_r, b_r, w_i, b_i):
    nb, gb, _ = w_r.shape
    groups = nb // GATE_GROUP
    def dense(w):
        w = (0.5 * w).astype(BF16).reshape(groups, GATE_GROUP, gb, gb)
        cols = [jnp.where((jnp.arange(GATE_GROUP) == k)[None, :, None, None], w, 0.0)
                .reshape(groups, GATE_GROUP * gb, gb) for k in range(GATE_GROUP)]
        return jnp.concatenate(cols, axis=2)

    w = jnp.concatenate([dense(w_r), dense(w_i)], axis=2)
    cg = GATE_GROUP * gb
    b = 0.5 * jnp.concatenate([b_r.reshape(groups, 1, cg), b_i.reshape(groups, 1, cg)], axis=2)
    return w, b.astype(F32)


def _attn_kernel(lam_ref, g_ref, q_ref, k_ref, v_ref, o_ref, acc, *, tq, lambda_init):
    qi = pl.program_id(2)
    hw = 2 * HEAD_DIM
    n_chains = acc.shape[0]
    chains = range(n_chains)
    q = q_ref[0]
    q_parts = [q[:, c * HEAD_DIM:(c + 1) * HEAD_DIM] for c in chains]
    acc[...] = jnp.zeros(acc.shape, F32)

    def step(first_block, n_blocks, carry, masked):
        width = n_blocks * tq
        start = pl.multiple_of(first_block * tq, tq)
        k = k_ref[0, pl.ds(start, width), :]
        v = v_ref[0, pl.ds(start, width), :]
        s = [lax.dot_general(q_parts[c], k[:, c * HEAD_DIM:(c + 1) * HEAD_DIM],
                             (((1,), (1,)), ((), ())), preferred_element_type=F32)
             for c in chains]
        if masked:
            row = lax.broadcasted_iota(jnp.int32, s[0].shape, 0)
            col = lax.broadcasted_iota(jnp.int32, s[0].shape, 1)
            keep = col - (n_blocks - 1) * tq <= row
            s = [jnp.where(keep, s[c], -jnp.inf) for c in chains]
        m_new = [jnp.maximum(carry[2 * c], jnp.max(s[c], axis=-1, keepdims=True)) for c in chains]
        p = [jnp.exp2(s[c] - m_new[c]) for c in chains]
        alpha = [jnp.exp2(carry[2 * c] - m_new[c]) for c in chains]
        l_new = [alpha[c] * carry[2 * c + 1] + jnp.sum(p[c], axis=-1, keepdims=True)
                 for c in chains]
        pv = [jnp.dot(p[c].astype(BF16), v[:, (c // 2) * hw:(c // 2 + 1) * hw],
                      preferred_element_type=F32) for c in chains]
        for c in chains:
            acc[c] = alpha[c] * acc[c] + pv[c]
        out = []
        for c in chains:
            out += [m_new[c], l_new[c]]
        return tuple(out)

    neg = jnp.full((tq, 1), -jnp.inf, F32)
    zero = jnp.zeros((tq, 1), F32)
    n_wide = qi // KV_BLOCKS_PER_STEP
    carry = lax.fori_loop(
        0, n_wide,
        lambda i, cr: step(i * KV_BLOCKS_PER_STEP, KV_BLOCKS_PER_STEP, cr, False),
        (neg, zero) * n_chains)
    rest = qi - n_wide * KV_BLOCKS_PER_STEP
    tails = [functools.partial(step, qi - r, r + 1, masked=True)
             for r in range(KV_BLOCKS_PER_STEP)]
    carry = lax.switch(rest, tails, carry)

    lv = lam_ref[0]
    lam = (jnp.exp(jnp.sum(lv[0:1] * lv[1:2], axis=-1, keepdims=True))
           - jnp.exp(jnp.sum(lv[2:3] * lv[3:4], axis=-1, keepdims=True)) + lambda_init)
    for head in range(n_chains // 2):
        c0, c1 = 2 * head, 2 * head + 1
        o = acc[c0] * (1.0 / carry[2 * c0 + 1]) - acc[c1] * (lam / carry[2 * c1 + 1])
        o = o * _rms_scale(o, SUBLN_EPS) * g_ref[...] * (1.0 - lambda_init)
        o_ref[0, :, head * hw:(head + 1) * hw] = o.astype(o_ref.dtype)


def _diff_attention(q, k, v, lam_vecs, subln_g, lambda_init):
    b, l, w = q.shape
    hw = 2 * HEAD_DIM
    bw = HEADS_PER_STEP * hw
    n_chains = 2 * HEADS_PER_STEP
    tq = SEQ_TILE
    kv_spec = pl.BlockSpec((1, l, bw), lambda bi, hi, qi: (bi, 0, hi))
    q_spec = pl.BlockSpec((1, tq, bw), lambda bi, hi, qi: (bi, qi, hi))
    return pl.pallas_call(
        functools.partial(_attn_kernel, tq=tq, lambda_init=lambda_init),
        grid=(b, w // bw, l // tq),
        in_specs=[
            pl.BlockSpec((1, 4, HEAD_DIM), lambda bi, hi, qi: (0, 0, 0)),
            pl.BlockSpec((1, hw), lambda bi, hi, qi: (0, 0)),
            q_spec, kv_spec, kv_spec,
        ],
        out_specs=q_spec,
        out_shape=jax.ShapeDtypeStruct((b, l, w), BF16),
        scratch_shapes=[pltpu.VMEM((n_chains, tq, hw), F32)],
        compiler_params=_params("parallel", "parallel", "arbitrary"),
    )(lam_vecs.reshape(1, 4, HEAD_DIM), subln_g.reshape(1, hw), q, k, v)


def _rope_tables(length, batch):
    inv = 1.0 / (ROPE_THETA ** (jnp.arange(0, HEAD_DIM, 2, dtype=F32) / HEAD_DIM))
    ang = jnp.arange(length, dtype=F32)[:, None] * inv[None, :]
    cos, sin = jnp.cos(ang), jnp.sin(ang)
    cos2 = jnp.concatenate([cos, cos], axis=1)
    sin2 = jnp.concatenate([-sin, sin], axis=1)
    return jnp.tile(cos2, (batch, 1)), jnp.tile(sin2, (batch, 1))


def kernel(x, meta_tokens, a_norm_g, a_w_in, a_conv_w, a_conv_b, a_w_r, a_b_r, a_w_i, a_b_i,
           a_lambda, a_w_out, kv_norm_g, w_kv, b_norm_g, b_w_q, b_lambda, b_subln_g, b_w_o,
           mlp_norm_g, mlp_w1, mlp_w2, final_norm_g):
    batch, seq, d = x.shape
    n_a = a_w_in.shape[0]
    depth = mlp_w1.shape[0]
    d_rnn = a_w_out.shape[1]
    qk_width = b_w_q.shape[2]
    v_width = w_kv.shape[1] - qk_width
    length = N_META_TOKENS + seq
    lp = pl.cdiv(length, SEQ_TILE) * SEQ_TILE
    t = batch * lp
    assert t % TOKEN_TILE == 0

    meta = jnp.broadcast_to(meta_tokens[None].astype(x.dtype), (batch, N_META_TOKENS, d))
    pad = jnp.zeros((batch, lp - length, d), x.dtype)
    h = jnp.concatenate([meta, x, pad], axis=1).reshape(t, d)
    rope = _rope_tables(lp, batch)

    w_in, w_out = a_w_in.astype(BF16), a_w_out.astype(BF16)
    wkv, w_q, w_o = w_kv.astype(BF16)[None], b_w_q.astype(BF16), b_w_o.astype(BF16)
    q_scale = math.log2(math.e) / math.sqrt(HEAD_DIM)

    k_sh = v_sh = None
    for layer in range(depth):
        if layer < n_a:
            j = layer
            gate = _norm_matmul(h, a_norm_g[j], w_in, j, 0, d_rnn, BF16, act="gelu")
            xr = _norm_matmul(h, a_norm_g[j], w_in, j, d_rnn, d_rnn, F32)
            w_g, b_g = _gate_weights(a_w_r[j], a_b_r[j], a_w_i[j], a_b_i[j])
            y = _rglru(xr.reshape(batch, lp, d_rnn), gate.reshape(batch, lp, d_rnn),
                       a_conv_w[j], a_conv_b[j], w_g, b_g, a_lambda[j])
            h = _matmul_residual(y.reshape(t, d_rnn), w_out, j, h)
        else:
            j = layer - n_a
            if j == 0:
                k_sh = _norm_matmul(h, kv_norm_g, wkv, 0, 0, qk_width, BF16, rope=rope)
                v_sh = _norm_matmul(h, kv_norm_g, wkv, 0, qk_width, v_width, BF16)
                k_sh = k_sh.reshape(batch, lp, qk_width)
                v_sh = v_sh.reshape(batch, lp, v_width)
            lambda_init = 0.8 - 0.6 * math.exp(-0.3 * layer)
            q = _norm_matmul(h, b_norm_g[j], w_q, j, 0, qk_width, BF16, rope=rope, scale=q_scale)
            o = _diff_attention(q.reshape(batch, lp, qk_width), k_sh, v_sh, b_lambda[j],
                                b_subln_g[j], lambda_init)
            h = _matmul_residual(o.reshape(t, v_width), w_o, j, h)
        final = (final_norm_g, batch, lp, length) if layer == depth - 1 else None
        h = _mlp(h, mlp_norm_g[layer], mlp_w1, mlp_w2, layer, final)
    return h.reshape(batch, seq, d)
```

```python
import functools
import math

import jax
import jax.numpy as jnp
from jax import lax
from jax.experimental import pallas as pl
from jax.experimental.pallas import tpu as pltpu

F32 = jnp.float32
BF16 = jnp.bfloat16

N_META_TOKENS = 16
CONV_WIDTH = 4
LRU_C = 8.0
HEAD_DIM = 128
ROPE_THETA = 10000.0
NORM_EPS = 1e-6
SUBLN_EPS = 1e-5

LANES = 128
SUBLANES = 8
VMEM_LIMIT_BYTES = 56 * 1024 * 1024

SEQ_TILE = 384
TOKEN_TILE = 768
FF_TILE = 512
GATE_GROUP = 4
SCAN_ROWS = 264
RGLRU_TILE = 1056
KV_BLOCKS_PER_STEP = 2
HEADS_PER_STEP = 2

def _params(*semantics):
    return pltpu.CompilerParams(dimension_semantics=semantics,
                                vmem_limit_bytes=VMEM_LIMIT_BYTES)


def _rms_scale(x, eps):
    return lax.rsqrt(jnp.mean(x * x, axis=-1, keepdims=True) + eps)


def _norm_matmul_kernel(*refs, act, rope, scale):
    if rope:
        x_ref, g_ref, w_ref, cos_ref, sin_ref, o_ref = refs
    else:
        x_ref, g_ref, w_ref, o_ref = refs

    x = x_ref[...]
    xn = (x * _rms_scale(x, NORM_EPS) * g_ref[...]).astype(BF16)
    y = jnp.dot(xn, w_ref[...], preferred_element_type=F32)
    if act == "gelu":
        y = jax.nn.gelu(y)
    if rope:
        c = cos_ref[...]
        s = sin_ref[...]
        segs = []
        for i in range(y.shape[1] // HEAD_DIM):
            seg = y[:, i * HEAD_DIM:(i + 1) * HEAD_DIM]
            segs.append(seg * c + pltpu.roll(seg, HEAD_DIM // 2, axis=1) * s)
        y = jnp.concatenate(segs, axis=1)
    if scale != 1.0:
        y = y * scale
    o_ref[...] = y.astype(o_ref.dtype)


def _norm_matmul(x, g, w, layer, col0, n, out_dtype, act=None, rope=None, scale=1.0):
    t, d = x.shape
    tm = TOKEN_TILE
    assert col0 % n == 0
    j0 = col0 // n
    in_specs = [
        pl.BlockSpec((tm, d), lambda i: (i, 0)),
        pl.BlockSpec((1, d), lambda i: (0, 0)),
        pl.BlockSpec((None, d, n), lambda i: (layer, 0, j0), pipeline_mode=pl.Buffered(1)),
    ]
    args = [x, g.reshape(1, d), w]
    if rope is not None:
        in_specs += [pl.BlockSpec((tm, HEAD_DIM), lambda i: (i, 0))] * 2
        args += list(rope)
    return pl.pallas_call(
        functools.partial(_norm_matmul_kernel, act=act, rope=rope is not None, scale=scale),
        grid=(t // tm,),
        in_specs=in_specs,
        out_specs=pl.BlockSpec((tm, n), lambda i: (i, 0)),
        out_shape=jax.ShapeDtypeStruct((t, n), out_dtype),
        compiler_params=_params("parallel"),
    )(*args)


def _matmul_residual_kernel(y_ref, w_ref, r_ref, o_ref):
    o_ref[...] = r_ref[...] + jnp.dot(y_ref[...], w_ref[...], preferred_element_type=F32)


def _matmul_residual(y, w, layer, res):
    t, k = y.shape
    n = w.shape[2]
    tm = TOKEN_TILE
    return pl.pallas_call(
        _matmul_residual_kernel,
        grid=(t // tm,),
        in_specs=[
            pl.BlockSpec((tm, k), lambda i: (i, 0)),
            pl.BlockSpec((None, k, n), lambda i: (layer, 0, 0), pipeline_mode=pl.Buffered(1)),
            pl.BlockSpec((tm, n), lambda i: (i, 0)),
        ],
        out_specs=pl.BlockSpec((tm, n), lambda i: (i, 0)),
        out_shape=jax.ShapeDtypeStruct((t, n), F32),
        compiler_params=_params("parallel"),
    )(y, w, res)


def _mlp_kernel(*refs, segments):
    final = segments is not None
    if final:
        x_ref, g_ref, w1_ref, w2_ref, fg_ref, o_ref, out_hbm, xn_ref, stage, sems = refs
    else:
        x_ref, g_ref, w1_ref, w2_ref, o_ref, xn_ref = refs
    i = pl.program_id(0)
    j = pl.program_id(1)

    @pl.when(j == 0)
    def _():
        x = x_ref[...]
        xn_ref[...] = (x * _rms_scale(x, NORM_EPS) * g_ref[...]).astype(xn_ref.dtype)
        o_ref[...] = x

    h = jnp.dot(xn_ref[...], w1_ref[...].astype(BF16), preferred_element_type=F32)
    h = jnp.square(jnp.maximum(h, 0.0)).astype(BF16)
    o_ref[...] += jnp.dot(h, w2_ref[...].astype(BF16), preferred_element_type=F32)

    if final:
        def copies(tile):
            return [pltpu.make_async_copy(stage.at[pl.ds(row, count)],
                                          out_hbm.at[pl.ds(dst, count)], sems.at[n])
                    for n, (row, count, dst) in enumerate(segments[tile])]

        def per_tile(action):
            for tile in range(len(segments)):
                pl.when(i == tile)(functools.partial(action, tile))

        def wait_previous(tile):
            if tile > 0:
                for cp in copies(tile - 1):
                    cp.wait()

        def start_current(tile):
            for cp in copies(tile):
                cp.start()
            if tile == len(segments) - 1:
                for cp in copies(tile):
                    cp.wait()

        @pl.when(j == pl.num_programs(1) - 1)
        def _():
            per_tile(wait_previous)
            y = o_ref[...]
            stage[...] = y * _rms_scale(y, NORM_EPS) * fg_ref[...]
            per_tile(start_current)


def _output_segments(n_tiles, tm, lp, length, seq):
    table = []
    for tile in range(n_tiles):
        first = tile * tm
        pieces = []
        for b in range(first // lp, (first + tm - 1) // lp + 1):
            lo = max(first, b * lp + N_META_TOKENS)
            hi = min(first + tm, b * lp + length)
            if hi > lo:
                pieces.append((lo - first, hi - lo, b * seq + lo - b * lp - N_META_TOKENS))
        table.append(tuple(pieces))
    return tuple(table)


def _mlp(x, g, w1, w2, layer, final=None):
    t, d = x.shape
    f = w1.shape[2]
    tm, tf = TOKEN_TILE, FF_TILE
    in_specs = [
        pl.BlockSpec((tm, d), lambda i, j: (i, 0)),
        pl.BlockSpec((1, d), lambda i, j: (0, 0)),
        pl.BlockSpec((None, d, tf), lambda i, j: (layer, 0, j)),
        pl.BlockSpec((None, tf, d), lambda i, j: (layer, j, 0)),
    ]
    args = [x, g.reshape(1, d), w1, w2]
    out_specs = pl.BlockSpec((tm, d), lambda i, j: (i, 0))
    out_shape = jax.ShapeDtypeStruct((t, d), F32)
    scratch_shapes = [pltpu.VMEM((tm, d), BF16)]
    if final is None:
        return pl.pallas_call(
            functools.partial(_mlp_kernel, segments=None),
            grid=(t // tm, f // tf),
            in_specs=in_specs,
            out_specs=out_specs,
            out_shape=out_shape,
            scratch_shapes=scratch_shapes,
            compiler_params=_params("parallel", "arbitrary"),
        )(*args)
    final_g, batch, lp, length = final
    seq = length - N_META_TOKENS
    segments = _output_segments(t // tm, tm, lp, length, seq)
    assert all(row % SUBLANES == 0 and count % SUBLANES == 0 and dst % SUBLANES == 0
               for pieces in segments for row, count, dst in pieces)
    max_pieces = max(len(pieces) for pieces in segments)
    _, out = pl.pallas_call(
        functools.partial(_mlp_kernel, segments=segments),
        grid=(t // tm, f // tf),
        in_specs=in_specs + [pl.BlockSpec((1, d), lambda i, j: (0, 0))],
        out_specs=[out_specs, pl.BlockSpec(memory_space=pl.ANY)],
        out_shape=[out_shape, jax.ShapeDtypeStruct((batch * seq, d), F32)],
        scratch_shapes=scratch_shapes + [pltpu.VMEM((tm, d), F32),
                                         pltpu.SemaphoreType.DMA((max_pieces,))],
        compiler_params=_params("arbitrary", "arbitrary"),
    )(*args, final_g.reshape(1, d))
    return out


def _rglru_kernel(xr_ref, gate_ref, cw_ref, cb_ref, w_ref, b_ref, lam_ref, y_ref,
                  xl, xc_scr, hl, aend, bend, cin, hcar, *, ts):
    c = xr_ref.shape[2]
    n_lane_tiles = c // LANES
    lane_tiles = range(n_lane_tiles)
    seg = SCAN_ROWS // SUBLANES
    n_groups = ts // SCAN_ROWS
    hdr = SUBLANES

    @pl.when(pl.program_id(2) == 0)
    def _():
        xl[:, 0:hdr, :] = jnp.zeros((n_lane_tiles, hdr, LANES), F32)
        hcar[...] = jnp.zeros(hcar.shape, F32)

    for j in lane_tiles:
        xl[j, hdr:hdr + ts, :] = xr_ref[0, :, j * LANES:(j + 1) * LANES]

    def lane(x, j):
        return x[:, j * LANES:(j + 1) * LANES]

    taps = [cw_ref[tap] for tap in range(CONV_WIDTH)]
    bias = cb_ref[...]
    for g in range(n_groups):
        loads = {}
        for k in range(-(CONV_WIDTH - 1), seg):
            loads[k] = xl[:, pl.ds(hdr + g * SCAN_ROWS + k, SUBLANES, stride=seg), :]
        for k in range(seg):
            xc = loads[k] * taps[CONV_WIDTH - 1] + bias
            for back in range(1, CONV_WIDTH):
                xc = xc + loads[k - back] * taps[CONV_WIDTH - 1 - back]
            r0 = g * SCAN_ROWS + k * SUBLANES
            for j in lane_tiles:
                xc_scr[r0:r0 + SUBLANES, j * LANES:(j + 1) * LANES] = xc[j]
    xl[:, 0:hdr, :] = xl[:, ts:ts + hdr, :]

    xc = xc_scr[...]
    half_gates = jnp.dot(xc.astype(BF16), w_ref[0], preferred_element_type=F32) + b_ref[0]
    t_r = jnp.tanh(half_gates[:, :c])
    t_i = jnp.tanh(half_gates[:, c:])
    half_rate = (-0.5 * LRU_C * math.log2(math.e)) * jax.nn.softplus(-lam_ref[...])
    a = jnp.exp2(t_r * half_rate + half_rate)
    bv = jnp.exp(0.5 * jnp.log(1.0 - a * a)) * ((0.5 * t_i + 0.5) * xc)

    def rows(x, g, k):
        r0 = g * SCAN_ROWS + k * SUBLANES
        return x[r0:r0 + SUBLANES, :]

    for g in range(n_groups):
        a_cum, b_cum = rows(a, g, 0), rows(bv, g, 0)
        for k in range(1, seg):
            a_k = rows(a, g, k)
            b_cum = a_k * b_cum + rows(bv, g, k)
            a_cum = a_k * a_cum
        aend[g * SUBLANES:(g + 1) * SUBLANES, :] = a_cum
        bend[g * SUBLANES:(g + 1) * SUBLANES, :] = b_cum

    carry = hcar[0:1, :]
    for s in range(n_groups * SUBLANES):
        cin[s:s + 1, :] = carry
        carry = aend[s:s + 1, :] * carry + bend[s:s + 1, :]
    hcar[0:1, :] = carry

    for g in range(n_groups):
        h = cin[g * SUBLANES:(g + 1) * SUBLANES, :]
        for k in range(seg):
            h = rows(a, g, k) * h + rows(bv, g, k)
            for j in lane_tiles:
                hl[j, pl.ds(g * SCAN_ROWS + k, SUBLANES, stride=seg), :] = lane(h, j)
    for j in lane_tiles:
        lanes = slice(j * LANES, (j + 1) * LANES)
        y_ref[0, :, lanes] = (hl[j] * gate_ref[0, :, lanes].astype(F32)).astype(y_ref.dtype)


def _rglru(xr, gate, conv_w, conv_b, w_gates, b_gates, lam):
    b, l, c = xr.shape
    groups = w_gates.shape[0]
    cg = c // groups
    nl = cg // LANES
    ts = RGLRU_TILE
    assert l % ts == 0
    n_runs = ts // SCAN_ROWS * SUBLANES
    seq_spec = pl.BlockSpec((1, ts, cg), lambda bi, gi, ci: (bi, ci, gi))
    return pl.pallas_call(
        functools.partial(_rglru_kernel, ts=ts),
        grid=(b, groups, l // ts),
        in_specs=[
            seq_spec, seq_spec,
            pl.BlockSpec((CONV_WIDTH, nl, 1, LANES), lambda bi, gi, ci: (0, gi, 0, 0)),
            pl.BlockSpec((nl, 1, LANES), lambda bi, gi, ci: (gi, 0, 0)),
            pl.BlockSpec((1, cg, 2 * cg), lambda bi, gi, ci: (gi, 0, 0)),
            pl.BlockSpec((1, 1, 2 * cg), lambda bi, gi, ci: (gi, 0, 0)),
            pl.BlockSpec((1, cg), lambda bi, gi, ci: (0, gi)),
        ],
        out_specs=seq_spec,
        out_shape=jax.ShapeDtypeStruct((b, l, c), BF16),
        scratch_shapes=[
            pltpu.VMEM((nl, SUBLANES + ts, LANES), F32),
            pltpu.VMEM((ts, cg), F32),
            pltpu.VMEM((nl, ts, LANES), F32),
            pltpu.VMEM((n_runs, cg), F32),
            pltpu.VMEM((n_runs, cg), F32),
            pltpu.VMEM((n_runs, cg), F32),
            pltpu.VMEM((SUBLANES, cg), F32),
        ],
        compiler_params=_params("parallel", "parallel", "arbitrary"),
    )(xr, gate, conv_w.reshape(CONV_WIDTH, c // LANES, 1, LANES),
      conv_b.reshape(c // LANES, 1, LANES), w_gates, b_gates, lam.reshape(1, c))


def _gate_weights(w_r, b_r, w_i, b_i):
    nb, gb, _ = w_r.shape
    groups = nb // GATE_GROUP
    def dense(w):
        w = (0.5 * w).astype(BF16).reshape(groups, GATE_GROUP, gb, gb)
        cols = [jnp.where((jnp.arange(GATE_GROUP) == k)[None, :, None, None], w, 0.0)
                .reshape(groups, GATE_GROUP * gb, gb) for k in range(GATE_GROUP)]
        return jnp.concatenate(cols, axis=2)

    w = jnp.concatenate([dense(w_r), dense(w_i)], axis=2)
    cg = GATE_GROUP * gb
    b = 0.5 * jnp.concatenate([b_r.reshape(groups, 1, cg), b_i.reshape(groups, 1, cg)], axis=2)
    return w, b.astype(F32)


def _attn_kernel(lam_ref, g_ref, q_ref, k_ref, v_ref, o_ref, acc, *, tq, lambda_init):
    qi = pl.program_id(2)
    hw = 2 * HEAD_DIM
    n_chains = acc.shape[0]
    chains = range(n_chains)
    q = q_ref[0]
    q_parts = [q[:, c * HEAD_DIM:(c + 1) * HEAD_DIM] for c in chains]
    acc[...] = jnp.zeros(acc.shape, F32)

    def step(first_block, n_blocks, carry, masked):
        width = n_blocks * tq
        start = pl.multiple_of(first_block * tq, tq)
        k = k_ref[0, pl.ds(start, width), :]
        v = v_ref[0, pl.ds(start, width), :]
        s = [lax.dot_general(q_parts[c], k[:, c * HEAD_DIM:(c + 1) * HEAD_DIM],
                             (((1,), (1,)), ((), ())), preferred_element_type=F32)
             for c in chains]
        if masked:
            row = lax.broadcasted_iota(jnp.int32, s[0].shape, 0)
            col = lax.broadcasted_iota(jnp.int32, s[0].shape, 1)
            keep = col - (n_blocks - 1) * tq <= row
            s = [jnp.where(keep, s[c], -jnp.inf) for c in chains]
        m_new = [jnp.maximum(carry[2 * c], jnp.max(s[c], axis=-1, keepdims=True)) for c in chains]
        p = [jnp.exp2(s[c] - m_new[c]) for c in chains]
        alpha = [jnp.exp2(carry[2 * c] - m_new[c]) for c in chains]
        l_new = [alpha[c] * carry[2 * c + 1] + jnp.sum(p[c], axis=-1, keepdims=True)
                 for c in chains]
        pv = [jnp.dot(p[c].astype(BF16), v[:, (c // 2) * hw:(c // 2 + 1) * hw],
                      preferred_element_type=F32) for c in chains]
        for c in chains:
            acc[c] = alpha[c] * acc[c] + pv[c]
        out = []
        for c in chains:
            out += [m_new[c], l_new[c]]
        return tuple(out)

    neg = jnp.full((tq, 1), -jnp.inf, F32)
    zero = jnp.zeros((tq, 1), F32)
    n_wide = qi // KV_BLOCKS_PER_STEP
    def wide(block, cr):
        return step(block, KV_BLOCKS_PER_STEP, cr, False)

    carry = lax.fori_loop(
        0, n_wide // 2,
        lambda i, cr: wide((2 * i + 1) * KV_BLOCKS_PER_STEP, wide(2 * i * KV_BLOCKS_PER_STEP, cr)),
        (neg, zero) * n_chains)
    carry = lax.cond(n_wide % 2 == 1,
                     lambda cr: wide((n_wide - 1) * KV_BLOCKS_PER_STEP, cr),
                     lambda cr: cr, carry)
    rest = qi - n_wide * KV_BLOCKS_PER_STEP
    tails = [functools.partial(step, qi - r, r + 1, masked=True)
             for r in range(KV_BLOCKS_PER_STEP)]
    carry = lax.switch(rest, tails, carry)

    lv = lam_ref[0]
    lam = (jnp.exp(jnp.sum(lv[0:1] * lv[1:2], axis=-1, keepdims=True))
           - jnp.exp(jnp.sum(lv[2:3] * lv[3:4], axis=-1, keepdims=True)) + lambda_init)
    for head in range(n_chains // 2):
        c0, c1 = 2 * head, 2 * head + 1
        o = acc[c0] * (1.0 / carry[2 * c0 + 1]) - acc[c1] * (lam / carry[2 * c1 + 1])
        o = o * _rms_scale(o, SUBLN_EPS) * g_ref[...] * (1.0 - lambda_init)
        o_ref[0, :, head * hw:(head + 1) * hw] = o.astype(o_ref.dtype)


def _diff_attention(q, k, v, lam_vecs, subln_g, lambda_init):
    b, l, w = q.shape
    hw = 2 * HEAD_DIM
    bw = HEADS_PER_STEP * hw
    n_chains = 2 * HEADS_PER_STEP
    tq = SEQ_TILE
    kv_spec = pl.BlockSpec((1, l, bw), lambda bi, hi, qi: (bi, 0, hi))
    q_spec = pl.BlockSpec((1, tq, bw), lambda bi, hi, qi: (bi, qi, hi))
    return pl.pallas_call(
        functools.partial(_attn_kernel, tq=tq, lambda_init=lambda_init),
        grid=(b, w // bw, l // tq),
        in_specs=[
            pl.BlockSpec((1, 4, HEAD_DIM), lambda bi, hi, qi: (0, 0, 0)),
            pl.BlockSpec((1, hw), lambda bi, hi, qi: (0, 0)),
            q_spec, kv_spec, kv_spec,
        ],
        out_specs=q_spec,
        out_shape=jax.ShapeDtypeStruct((b, l, w), BF16),
        scratch_shapes=[pltpu.VMEM((n_chains, tq, hw), F32)],
        compiler_params=_params("parallel", "parallel", "arbitrary"),
    )(lam_vecs.reshape(1, 4, HEAD_DIM), subln_g.reshape(1, hw), q, k, v)


def _rope_tables(length, batch):
    inv = 1.0 / (ROPE_THETA ** (jnp.arange(0, HEAD_DIM, 2, dtype=F32) / HEAD_DIM))
    ang = jnp.arange(length, dtype=F32)[:, None] * inv[None, :]
    cos, sin = jnp.cos(ang), jnp.sin(ang)
    cos2 = jnp.concatenate([cos, cos], axis=1)
    sin2 = jnp.concatenate([-sin, sin], axis=1)
    return jnp.tile(cos2, (batch, 1)), jnp.tile(sin2, (batch, 1))


def kernel(x, meta_tokens, a_norm_g, a_w_in, a_conv_w, a_conv_b, a_w_r, a_b_r, a_w_i, a_b_i,
           a_lambda, a_w_out, kv_norm_g, w_kv, b_norm_g, b_w_q, b_lambda, b_subln_g, b_w_o,
           mlp_norm_g, mlp_w1, mlp_w2, final_norm_g):
    batch, seq, d = x.shape
    n_a = a_w_in.shape[0]
    depth = mlp_w1.shape[0]
    d_rnn = a_w_out.shape[1]
    qk_width = b_w_q.shape[2]
    v_width = w_kv.shape[1] - qk_width
    length = N_META_TOKENS + seq
    lp = pl.cdiv(length, SEQ_TILE) * SEQ_TILE
    t = batch * lp
    assert t % TOKEN_TILE == 0

    meta = jnp.broadcast_to(meta_tokens[None].astype(x.dtype), (batch, N_META_TOKENS, d))
    pad = jnp.zeros((batch, lp - length, d), x.dtype)
    h = jnp.concatenate([meta, x, pad], axis=1).reshape(t, d)
    rope = _rope_tables(lp, batch)

    w_in, w_out = a_w_in.astype(BF16), a_w_out.astype(BF16)
    wkv, w_q, w_o = w_kv.astype(BF16)[None], b_w_q.astype(BF16), b_w_o.astype(BF16)
    q_scale = math.log2(math.e) / math.sqrt(HEAD_DIM)

    k_sh = v_sh = None
    for layer in range(depth):
        if layer < n_a:
            j = layer
            gate = _norm_matmul(h, a_norm_g[j], w_in, j, 0, d_rnn, BF16, act="gelu")
            xr = _norm_matmul(h, a_norm_g[j], w_in, j, d_rnn, d_rnn, F32)
            w_g, b_g = _gate_weights(a_w_r[j], a_b_r[j], a_w_i[j], a_b_i[j])
            y = _rglru(xr.reshape(batch, lp, d_rnn), gate.reshape(batch, lp, d_rnn),
                       a_conv_w[j], a_conv_b[j], w_g, b_g, a_lambda[j])
            h = _matmul_residual(y.reshape(t, d_rnn), w_out, j, h)
        else:
            j = layer - n_a
            if j == 0:
                k_sh = _norm_matmul(h, kv_norm_g, wkv, 0, 0, qk_width, BF16, rope=rope)
                v_sh = _norm_matmul(h, kv_norm_g, wkv, 0, qk_width, v_width, BF16)
                k_sh = k_sh.reshape(batch, lp, qk_width)
                v_sh = v_sh.reshape(batch, lp, v_width)
            lambda_init = 0.8 - 0.6 * math.exp(-0.3 * layer)
            q = _norm_matmul(h, b_norm_g[j], w_q, j, 0, qk_width, BF16, rope=rope, scale=q_scale)
            o = _diff_attention(q.reshape(batch, lp, qk_width), k_sh, v_sh, b_lambda[j],
                                b_subln_g[j], lambda_init)
            h = _matmul_residual(o.reshape(t, v_width), w_o, j, h)
        final = (final_norm_g, batch, lp, length) if layer == depth - 1 else None
        h = _mlp(h, mlp_norm_g[layer], mlp_w1, mlp_w2, layer, final)
    return h.reshape(batch, seq, d)
```

```python
import functools
import math

import jax
import jax.numpy as jnp
from jax import lax
from jax.experimental import pallas as pl
from jax.experimental.pallas import tpu as pltpu

F32 = jnp.float32
BF16 = jnp.bfloat16

N_META_TOKENS = 16
CONV_WIDTH = 4
LRU_C = 8.0
HEAD_DIM = 128
ROPE_THETA = 10000.0
NORM_EPS = 1e-6
SUBLN_EPS = 1e-5

LANES = 128
SUBLANES = 8
VMEM_LIMIT_BYTES = 56 * 1024 * 1024

SEQ_TILE = 384
TOKEN_TILE = 768
FF_TILE = 512
GATE_GROUP = 4
SCAN_ROWS = 264
RGLRU_TILE = 1056
KV_BLOCKS_PER_STEP = 2
HEADS_PER_STEP = 2


def _params(*semantics):
    return pltpu.CompilerParams(dimension_semantics=semantics,
                                vmem_limit_bytes=VMEM_LIMIT_BYTES)


def _rms_scale(x, eps):
    return lax.rsqrt(jnp.mean(x * x, axis=-1, keepdims=True) + eps)


def _norm_matmul_kernel(*refs, act, rope, scale):
    if rope:
        x_ref, g_ref, w_ref, cos_ref, sin_ref, o_ref = refs
    else:
        x_ref, g_ref, w_ref, o_ref = refs

    x = x_ref[...]
    xn = (x * _rms_scale(x, NORM_EPS) * g_ref[...]).astype(BF16)
    y = jnp.dot(xn, w_ref[...], preferred_element_type=F32)
    if act == "gelu":
        y = jax.nn.gelu(y)
    if rope:
        c = cos_ref[...]
        s = sin_ref[...]
        segs = []
        for i in range(y.shape[1] // HEAD_DIM):
            seg = y[:, i * HEAD_DIM:(i + 1) * HEAD_DIM]
            segs.append(seg * c + pltpu.roll(seg, HEAD_DIM // 2, axis=1) * s)
        y = jnp.concatenate(segs, axis=1)
    if scale != 1.0:
        y = y * scale
    o_ref[...] = y.astype(o_ref.dtype)


def _norm_matmul(x, g, w, layer, col0, n, out_dtype, act=None, rope=None, scale=1.0):
    t, d = x.shape
    tm = TOKEN_TILE
    assert col0 % n == 0
    j0 = col0 // n
    in_specs = [
        pl.BlockSpec((tm, d), lambda i: (i, 0)),
        pl.BlockSpec((1, d), lambda i: (0, 0)),
        pl.BlockSpec((None, d, n), lambda i: (layer, 0, j0), pipeline_mode=pl.Buffered(1)),
    ]
    args = [x, g.reshape(1, d), w]
    if rope is not None:
        in_specs += [pl.BlockSpec((tm, HEAD_DIM), lambda i: (i, 0))] * 2
        args += list(rope)
    return pl.pallas_call(
        functools.partial(_norm_matmul_kernel, act=act, rope=rope is not None, scale=scale),
        grid=(t // tm,),
        in_specs=in_specs,
        out_specs=pl.BlockSpec((tm, n), lambda i: (i, 0)),
        out_shape=jax.ShapeDtypeStruct((t, n), out_dtype),
        compiler_params=_params("parallel"),
    )(*args)


def _matmul_residual_kernel(y_ref, w_ref, r_ref, o_ref):
    o_ref[...] = r_ref[...] + jnp.dot(y_ref[...], w_ref[...], preferred_element_type=F32)


def _matmul_residual(y, w, layer, res):
    t, k = y.shape
    n = w.shape[2]
    tm = TOKEN_TILE
    return pl.pallas_call(
        _matmul_residual_kernel,
        grid=(t // tm,),
        in_specs=[
            pl.BlockSpec((tm, k), lambda i: (i, 0)),
            pl.BlockSpec((None, k, n), lambda i: (layer, 0, 0), pipeline_mode=pl.Buffered(1)),
            pl.BlockSpec((tm, n), lambda i: (i, 0)),
        ],
        out_specs=pl.BlockSpec((tm, n), lambda i: (i, 0)),
        out_shape=jax.ShapeDtypeStruct((t, n), F32),
        compiler_params=_params("parallel"),
    )(y, w, res)


def _mlp_kernel(*refs, segments):
    final = segments is not None
    if final:
        x_ref, g_ref, w1_ref, w2_ref, fg_ref, o_ref, out_hbm, xn_ref, stage, sems = refs
    else:
        x_ref, g_ref, w1_ref, w2_ref, o_ref, xn_ref = refs
    i = pl.program_id(0)
    j = pl.program_id(1)

    @pl.when(j == 0)
    def _():
        x = x_ref[...]
        xn_ref[...] = (x * _rms_scale(x, NORM_EPS) * g_ref[...]).astype(xn_ref.dtype)
        o_ref[...] = x

    h = jnp.dot(xn_ref[...], w1_ref[...].astype(BF16), preferred_element_type=F32)
    h = jnp.square(jnp.maximum(h, 0.0)).astype(BF16)
    o_ref[...] += jnp.dot(h, w2_ref[...].astype(BF16), preferred_element_type=F32)

    if final:
        def copies(tile):
            return [pltpu.make_async_copy(stage.at[pl.ds(row, count)],
                                          out_hbm.at[pl.ds(dst, count)], sems.at[n])
                    for n, (row, count, dst) in enumerate(segments[tile])]

        def per_tile(action):
            for tile in range(len(segments)):
                pl.when(i == tile)(functools.partial(action, tile))

        def wait_previous(tile):
            if tile > 0:
                for cp in copies(tile - 1):
                    cp.wait()

        def start_current(tile):
            for cp in copies(tile):
                cp.start()
            if tile == len(segments) - 1:
                for cp in copies(tile):
                    cp.wait()

        @pl.when(j == pl.num_programs(1) - 1)
        def _():
            per_tile(wait_previous)
            y = o_ref[...]
            stage[...] = y * _rms_scale(y, NORM_EPS) * fg_ref[...]
            per_tile(start_current)


def _output_segments(n_tiles, tm, lp, length, seq):
    table = []
    for tile in range(n_tiles):
        first = tile * tm
        pieces = []
        for b in range(first // lp, (first + tm - 1) // lp + 1):
            lo = max(first, b * lp + N_META_TOKENS)
            hi = min(first + tm, b * lp + length)
            if hi > lo:
                pieces.append((lo - first, hi - lo, b * seq + lo - b * lp - N_META_TOKENS))
        table.append(tuple(pieces))
    return tuple(table)


def _mlp(x, g, w1, w2, layer, final=None):
    t, d = x.shape
    f = w1.shape[2]
    tm, tf = TOKEN_TILE, FF_TILE
    in_specs = [
        pl.BlockSpec((tm, d), lambda i, j: (i, 0)),
        pl.BlockSpec((1, d), lambda i, j: (0, 0)),
        pl.BlockSpec((None, d, tf), lambda i, j: (layer, 0, j)),
        pl.BlockSpec((None, tf, d), lambda i, j: (layer, j, 0)),
    ]
    args = [x, g.reshape(1, d), w1, w2]
    out_specs = pl.BlockSpec((tm, d), lambda i, j: (i, 0))
    out_shape = jax.ShapeDtypeStruct((t, d), F32)
    scratch_shapes = [pltpu.VMEM((tm, d), BF16)]
    if final is None:
        return pl.pallas_call(
            functools.partial(_mlp_kernel, segments=None),
            grid=(t // tm, f // tf),
            in_specs=in_specs,
            out_specs=out_specs,
            out_shape=out_shape,
            scratch_shapes=scratch_shapes,
            compiler_params=_params("parallel", "arbitrary"),
        )(*args)
    final_g, batch, lp, length = final
    seq = length - N_META_TOKENS
    segments = _output_segments(t // tm, tm, lp, length, seq)
    assert all(row % SUBLANES == 0 and count % SUBLANES == 0 and dst % SUBLANES == 0
               for pieces in segments for row, count, dst in pieces)
    max_pieces = max(len(pieces) for pieces in segments)
    _, out = pl.pallas_call(
        functools.partial(_mlp_kernel, segments=segments),
        grid=(t // tm, f // tf),
        in_specs=in_specs + [pl.BlockSpec((1, d), lambda i, j: (0, 0))],
        out_specs=[out_specs, pl.BlockSpec(memory_space=pl.ANY)],
        out_shape=[out_shape, jax.ShapeDtypeStruct((batch * seq, d), F32)],
        scratch_shapes=scratch_shapes + [pltpu.VMEM((tm, d), F32),
                                         pltpu.SemaphoreType.DMA((max_pieces,))],
        compiler_params=_params("arbitrary", "arbitrary"),
    )(*args, final_g.reshape(1, d))
    return out


def _rglru_kernel(xr_ref, gate_ref, cw_ref, cb_ref, w_ref, b_ref, lam_ref, y_ref,
                  xl, xc_scr, hl, aend, bend, cin, hcar, *, ts):
    c = xr_ref.shape[2]
    n_lane_tiles = c // LANES
    lane_tiles = range(n_lane_tiles)
    seg = SCAN_ROWS // SUBLANES
    n_groups = ts // SCAN_ROWS
    hdr = SUBLANES

    @pl.when(pl.program_id(2) == 0)
    def _():
        xl[:, 0:hdr, :] = jnp.zeros((n_lane_tiles, hdr, LANES), F32)
        hcar[...] = jnp.zeros(hcar.shape, F32)

    for j in lane_tiles:
        xl[j, hdr:hdr + ts, :] = xr_ref[0, :, j * LANES:(j + 1) * LANES]

    def lane(x, j):
        return x[:, j * LANES:(j + 1) * LANES]

    taps = [cw_ref[tap] for tap in range(CONV_WIDTH)]
    bias = cb_ref[...]
    for g in range(n_groups):
        loads = {}
        for k in range(-(CONV_WIDTH - 1), seg):
            loads[k] = xl[:, pl.ds(hdr + g * SCAN_ROWS + k, SUBLANES, stride=seg), :]
        for k in range(seg):
            xc = loads[k] * taps[CONV_WIDTH - 1] + bias
            for back in range(1, CONV_WIDTH):
                xc = xc + loads[k - back] * taps[CONV_WIDTH - 1 - back]
            r0 = g * SCAN_ROWS + k * SUBLANES
            for j in lane_tiles:
                xc_scr[r0:r0 + SUBLANES, j * LANES:(j + 1) * LANES] = xc[j]
    xl[:, 0:hdr, :] = xl[:, ts:ts + hdr, :]

    xc = xc_scr[...]
    half_gates = jnp.dot(xc.astype(BF16), w_ref[0], preferred_element_type=F32) + b_ref[0]
    t_r = jnp.tanh(half_gates[:, :c])
    t_i = jnp.tanh(half_gates[:, c:])
    half_rate = (-0.5 * LRU_C * math.log2(math.e)) * jax.nn.softplus(-lam_ref[...])
    a = jnp.exp2(t_r * half_rate + half_rate)
    bv = jnp.exp(0.5 * jnp.log(1.0 - a * a)) * ((0.5 * t_i + 0.5) * xc)

    def rows(x, g, k):
        r0 = g * SCAN_ROWS + k * SUBLANES
        return x[r0:r0 + SUBLANES, :]

    for g in range(n_groups):
        a_cum, b_cum = rows(a, g, 0), rows(bv, g, 0)
        for k in range(1, seg):
            a_k = rows(a, g, k)
            b_cum = a_k * b_cum + rows(bv, g, k)
            a_cum = a_k * a_cum
        aend[g * SUBLANES:(g + 1) * SUBLANES, :] = a_cum
        bend[g * SUBLANES:(g + 1) * SUBLANES, :] = b_cum

    carry = hcar[0:1, :]
    for s in range(n_groups * SUBLANES):
        cin[s:s + 1, :] = carry
        carry = aend[s:s + 1, :] * carry + bend[s:s + 1, :]
    hcar[0:1, :] = carry

    for g in range(n_groups):
        h = cin[g * SUBLANES:(g + 1) * SUBLANES, :]
        for k in range(seg):
            h = rows(a, g, k) * h + rows(bv, g, k)
            for j in lane_tiles:
                hl[j, pl.ds(g * SCAN_ROWS + k, SUBLANES, stride=seg), :] = lane(h, j)
    for j in lane_tiles:
        lanes = slice(j * LANES, (j + 1) * LANES)
        y_ref[0, :, lanes] = (hl[j] * gate_ref[0, :, lanes].astype(F32)).astype(y_ref.dtype)


def _rglru(xr, gate, conv_w, conv_b, w_gates, b_gates, lam):
    b, l, c = xr.shape
    groups = w_gates.shape[0]
    cg = c // groups
    nl = cg // LANES
    ts = RGLRU_TILE
    assert l % ts == 0
    n_runs = ts // SCAN_ROWS * SUBLANES
    seq_spec = pl.BlockSpec((1, ts, cg), lambda bi, gi, ci: (bi, ci, gi))
    return pl.pallas_call(
        functools.partial(_rglru_kernel, ts=ts),
        grid=(b, groups, l // ts),
        in_specs=[
            seq_spec, seq_spec,
            pl.BlockSpec((CONV_WIDTH, nl, 1, LANES), lambda bi, gi, ci: (0, gi, 0, 0)),
            pl.BlockSpec((nl, 1, LANES), lambda bi, gi, ci: (gi, 0, 0)),
            pl.BlockSpec((1, cg, 2 * cg), lambda bi, gi, ci: (gi, 0, 0)),
            pl.BlockSpec((1, 1, 2 * cg), lambda bi, gi, ci: (gi, 0, 0)),
            pl.BlockSpec((1, cg), lambda bi, gi, ci: (0, gi)),
        ],
        out_specs=seq_spec,
        out_shape=jax.ShapeDtypeStruct((b, l, c), BF16),
        scratch_shapes=[
            pltpu.VMEM((nl, SUBLANES + ts, LANES), F32),
            pltpu.VMEM((ts, cg), F32),
            pltpu.VMEM((nl, ts, LANES), F32),
            pltpu.VMEM((n_runs, cg), F32),
            pltpu.VMEM((n_runs, cg), F32),
            pltpu.VMEM((n_runs, cg), F32),
            pltpu.VMEM((SUBLANES, cg), F32),
        ],
        compiler_params=_params("parallel", "parallel", "arbitrary"),
    )(xr, gate, conv_w.reshape(CONV_WIDTH, c // LANES, 1, LANES),
      conv_b.reshape(c // LANES, 1, LANES), w_gates, b_gates, lam.reshape(1, c))


def _gate_weights(w_r, b_r, w_i, b_i):
    nb, gb, _ = w_r.shape
    groups = nb // GATE_GROUP
    def dense(w):
        w = (0.5 * w).astype(BF16).reshape(groups, GATE_GROUP, gb, gb)
        cols = [jnp.where((jnp.arange(GATE_GROUP) == k)[None, :, None, None], w, 0.0)
                .reshape(groups, GATE_GROUP * gb, gb) for k in range(GATE_GROUP)]
        return jnp.concatenate(cols, axis=2)

    w = jnp.concatenate([dense(w_r), dense(w_i)], axis=2)
    cg = GATE_GROUP * gb
    b = 0.5 * jnp.concatenate([b_r.reshape(groups, 1, cg), b_i.reshape(groups, 1, cg)], axis=2)
    return w, b.astype(F32)


def _attn_kernel(lam_ref, g_ref, q_ref, k_ref, v_ref, o_ref, acc, *, tq, lambda_init):
    qi = pl.program_id(2)
    hw = 2 * HEAD_DIM
    n_chains = acc.shape[0]
    chains = range(n_chains)
    q = q_ref[0]
    q_parts = [q[:, c * HEAD_DIM:(c + 1) * HEAD_DIM] for c in chains]
    acc[...] = jnp.zeros(acc.shape, F32)

    def step(first_block, n_blocks, carry, masked):
        width = n_blocks * tq
        start = pl.multiple_of(first_block * tq, tq)
        k = k_ref[0, pl.ds(start, width), :]
        v = v_ref[0, pl.ds(start, width), :]
        s = [lax.dot_general(q_parts[c], k[:, c * HEAD_DIM:(c + 1) * HEAD_DIM],
                             (((1,), (1,)), ((), ())), preferred_element_type=F32)
             for c in chains]
        if masked:
            row = lax.broadcasted_iota(jnp.int32, s[0].shape, 0)
            col = lax.broadcasted_iota(jnp.int32, s[0].shape, 1)
            keep = col - (n_blocks - 1) * tq <= row
            s = [jnp.where(keep, s[c], -jnp.inf) for c in chains]
        m_new = [jnp.maximum(carry[2 * c], jnp.max(s[c], axis=-1, keepdims=True)) for c in chains]
        p = [jnp.exp2(s[c] - m_new[c]) for c in chains]
        alpha = [jnp.exp2(carry[2 * c] - m_new[c]) for c in chains]
        l_new = [alpha[c] * carry[2 * c + 1] + jnp.sum(p[c], axis=-1, keepdims=True)
                 for c in chains]
        pv = [jnp.dot(p[c].astype(BF16), v[:, (c // 2) * hw:(c // 2 + 1) * hw],
                      preferred_element_type=F32) for c in chains]
        for c in chains:
            acc[c] = alpha[c] * acc[c] + pv[c]
        out = []
        for c in chains:
            out += [m_new[c], l_new[c]]
        return tuple(out)

    neg = jnp.full((tq, 1), -jnp.inf, F32)
    zero = jnp.zeros((tq, 1), F32)
    n_wide = qi // KV_BLOCKS_PER_STEP
    def wide(block, cr):
        return step(block, KV_BLOCKS_PER_STEP, cr, False)

    carry = lax.fori_loop(
        0, n_wide // 2,
        lambda i, cr: wide((2 * i + 1) * KV_BLOCKS_PER_STEP, wide(2 * i * KV_BLOCKS_PER_STEP, cr)),
        (neg, zero) * n_chains)
    carry = lax.cond(n_wide % 2 == 1,
                     lambda cr: wide((n_wide - 1) * KV_BLOCKS_PER_STEP, cr),
                     lambda cr: cr, carry)
    rest = qi - n_wide * KV_BLOCKS_PER_STEP
    tails = [functools.partial(step, qi - r, r + 1, masked=True)
             for r in range(KV_BLOCKS_PER_STEP)]
    carry = lax.switch(rest, tails, carry)

    lv = lam_ref[0]
    lam = (jnp.exp(jnp.sum(lv[0:1] * lv[1:2], axis=-1, keepdims=True))
           - jnp.exp(jnp.sum(lv[2:3] * lv[3:4], axis=-1, keepdims=True)) + lambda_init)
    for head in range(n_chains // 2):
        c0, c1 = 2 * head, 2 * head + 1
        o = acc[c0] * (1.0 / carry[2 * c0 + 1]) - acc[c1] * (lam / carry[2 * c1 + 1])
        o = o * _rms_scale(o, SUBLN_EPS) * g_ref[...] * (1.0 - lambda_init)
        o_ref[0, :, head * hw:(head + 1) * hw] = o.astype(o_ref.dtype)


def _diff_attention(q, k, v, lam_vecs, subln_g, lambda_init):
    b, l, w = q.shape
    hw = 2 * HEAD_DIM
    bw = HEADS_PER_STEP * hw
    n_chains = 2 * HEADS_PER_STEP
    tq = SEQ_TILE
    kv_spec = pl.BlockSpec((1, l, bw), lambda bi, hi, qi: (bi, 0, hi))
    q_spec = pl.BlockSpec((1, tq, bw), lambda bi, hi, qi: (bi, qi, hi))
    return pl.pallas_call(
        functools.partial(_attn_kernel, tq=tq, lambda_init=lambda_init),
        grid=(b, w // bw, l // tq),
        in_specs=[
            pl.BlockSpec((1, 4, HEAD_DIM), lambda bi, hi, qi: (0, 0, 0)),
            pl.BlockSpec((1, hw), lambda bi, hi, qi: (0, 0)),
            q_spec, kv_spec, kv_spec,
        ],
        out_specs=q_spec,
        out_shape=jax.ShapeDtypeStruct((b, l, w), BF16),
        scratch_shapes=[pltpu.VMEM((n_chains, tq, hw), F32)],
        compiler_params=_params("parallel", "parallel", "arbitrary"),
    )(lam_vecs.reshape(1, 4, HEAD_DIM), subln_g.reshape(1, hw), q, k, v)


def _rope_tables(length, batch):
    inv = 1.0 / (ROPE_THETA ** (jnp.arange(0, HEAD_DIM, 2, dtype=F32) / HEAD_DIM))
    ang = jnp.arange(length, dtype=F32)[:, None] * inv[None, :]
    cos, sin = jnp.cos(ang), jnp.sin(ang)
    cos2 = jnp.concatenate([cos, cos], axis=1)
    sin2 = jnp.concatenate([-sin, sin], axis=1)
    return jnp.tile(cos2, (batch, 1)), jnp.tile(sin2, (batch, 1))


def kernel(x, meta_tokens, a_norm_g, a_w_in, a_conv_w, a_conv_b, a_w_r, a_b_r, a_w_i, a_b_i,
           a_lambda, a_w_out, kv_norm_g, w_kv, b_norm_g, b_w_q, b_lambda, b_subln_g, b_w_o,
           mlp_norm_g, mlp_w1, mlp_w2, final_norm_g):
    batch, seq, d = x.shape
    n_a = a_w_in.shape[0]
    depth = mlp_w1.shape[0]
    d_rnn = a_w_out.shape[1]
    qk_width = b_w_q.shape[2]
    v_width = w_kv.shape[1] - qk_width
    length = N_META_TOKENS + seq
    lp = pl.cdiv(length, SEQ_TILE) * SEQ_TILE
    t = batch * lp
    assert t % TOKEN_TILE == 0

    meta = jnp.broadcast_to(meta_tokens[None].astype(x.dtype), (batch, N_META_TOKENS, d))
    pad = jnp.zeros((batch, lp - length, d), x.dtype)
    h = jnp.concatenate([meta, x, pad], axis=1).reshape(t, d)
    rope = _rope_tables(lp, batch)

    w_in, w_out = a_w_in.astype(BF16), a_w_out.astype(BF16)
    wkv, w_q, w_o = w_kv.astype(BF16)[None], b_w_q.astype(BF16), b_w_o.astype(BF16)
    q_scale = math.log2(math.e) / math.sqrt(HEAD_DIM)

    k_sh = v_sh = None
    for layer in range(depth):
        if layer < n_a:
            j = layer
            gate = _norm_matmul(h, a_norm_g[j], w_in, j, 0, d_rnn, BF16, act="gelu")
            xr = _norm_matmul(h, a_norm_g[j], w_in, j, d_rnn, d_rnn, F32)
            w_g, b_g = _gate_weights(a_w_r[j], a_b_r[j], a_w_i[j], a_b_i[j])
            y = _rglru(xr.reshape(batch, lp, d_rnn), gate.reshape(batch, lp, d_rnn),
                       a_conv_w[j], a_conv_b[j], w_g, b_g, a_lambda[j])
            h = _matmul_residual(y.reshape(t, d_rnn), w_out, j, h)
        else:
            j = layer - n_a
            if j == 0:
                k_sh = _norm_matmul(h, kv_norm_g, wkv, 0, 0, qk_width, BF16, rope=rope)
                v_sh = _norm_matmul(h, kv_norm_g, wkv, 0, qk_width, v_width, BF16)
                k_sh = k_sh.reshape(batch, lp, qk_width)
                v_sh = v_sh.reshape(batch, lp, v_width)
            lambda_init = 0.8 - 0.6 * math.exp(-0.3 * layer)
            q = _norm_matmul(h, b_norm_g[j], w_q, j, 0, qk_width, BF16, rope=rope, scale=q_scale)
            o = _diff_attention(q.reshape(batch, lp, qk_width), k_sh, v_sh, b_lambda[j],
                                b_subln_g[j], lambda_init)
            h = _matmul_residual(o.reshape(t, v_width), w_o, j, h)
        final = (final_norm_g, batch, lp, length) if layer == depth - 1 else None
        h = _mlp(h, mlp_norm_g[layer], mlp_w1, mlp_w2, layer, final)
    return h.reshape(batch, seq, d)
```

```python
import functools
import math

import jax
import jax.numpy as jnp
from jax import lax
from jax.experimental import pallas as pl
from jax.experimental.pallas import tpu as pltpu

F32 = jnp.float32
BF16 = jnp.bfloat16

N_META_TOKENS = 16
CONV_WIDTH = 4
LRU_C = 8.0
HEAD_DIM = 128
ROPE_THETA = 10000.0
NORM_EPS = 1e-6
SUBLN_EPS = 1e-5

LANES = 128
SUBLANES = 8
VMEM_LIMIT_BYTES = 56 * 1024 * 1024

SEQ_TILE = 384
TOKEN_TILE = 768
FF_TILE = 512
MLP_TILE = 1056
GATE_GROUP = 4
SCAN_ROWS = 264
RGLRU_TILE = 1056
KV_BLOCKS_PER_STEP = 2
HEADS_PER_STEP = 2


def _params(*semantics):
    return pltpu.CompilerParams(dimension_semantics=semantics,
                                vmem_limit_bytes=VMEM_LIMIT_BYTES)


def _rms_scale(x, eps):
    return lax.rsqrt(jnp.mean(x * x, axis=-1, keepdims=True) + eps)


def _norm_matmul_kernel(*refs, act, rope, scale):
    if rope:
        x_ref, g_ref, w_ref, cos_ref, sin_ref, o_ref = refs
    else:
        x_ref, g_ref, w_ref, o_ref = refs

    x = x_ref[...]
    xn = (x * _rms_scale(x, NORM_EPS) * g_ref[...]).astype(BF16)
    y = jnp.dot(xn, w_ref[...], preferred_element_type=F32)
    if act == "gelu":
        y = jax.nn.gelu(y)
    if rope:
        c = cos_ref[...]
        s = sin_ref[...]
        segs = []
        for i in range(y.shape[1] // HEAD_DIM):
            seg = y[:, i * HEAD_DIM:(i + 1) * HEAD_DIM]
            segs.append(seg * c + pltpu.roll(seg, HEAD_DIM // 2, axis=1) * s)
        y = jnp.concatenate(segs, axis=1)
    if scale != 1.0:
        y = y * scale
    o_ref[...] = y.astype(o_ref.dtype)


def _norm_matmul(x, g, w, layer, col0, n, out_dtype, act=None, rope=None, scale=1.0):
    t, d = x.shape
    tm = TOKEN_TILE
    assert col0 % n == 0
    j0 = col0 // n
    in_specs = [
        pl.BlockSpec((tm, d), lambda i: (i, 0)),
        pl.BlockSpec((1, d), lambda i: (0, 0)),
        pl.BlockSpec((None, d, n), lambda i: (layer, 0, j0), pipeline_mode=pl.Buffered(1)),
    ]
    args = [x, g.reshape(1, d), w]
    if rope is not None:
        in_specs += [pl.BlockSpec((tm, HEAD_DIM), lambda i: (i, 0))] * 2
        args += list(rope)
    return pl.pallas_call(
        functools.partial(_norm_matmul_kernel, act=act, rope=rope is not None, scale=scale),
        grid=(t // tm,),
        in_specs=in_specs,
        out_specs=pl.BlockSpec((tm, n), lambda i: (i, 0)),
        out_shape=jax.ShapeDtypeStruct((t, n), out_dtype),
        compiler_params=_params("parallel"),
    )(*args)


def _matmul_residual_kernel(y_ref, w_ref, r_ref, o_ref):
    o_ref[...] = r_ref[...] + jnp.dot(y_ref[...], w_ref[...], preferred_element_type=F32)


def _matmul_residual(y, w, layer, res):
    t, k = y.shape
    n = w.shape[2]
    tm = TOKEN_TILE
    return pl.pallas_call(
        _matmul_residual_kernel,
        grid=(t // tm,),
        in_specs=[
            pl.BlockSpec((tm, k), lambda i: (i, 0)),
            pl.BlockSpec((None, k, n), lambda i: (layer, 0, 0), pipeline_mode=pl.Buffered(1)),
            pl.BlockSpec((tm, n), lambda i: (i, 0)),
        ],
        out_specs=pl.BlockSpec((tm, n), lambda i: (i, 0)),
        out_shape=jax.ShapeDtypeStruct((t, n), F32),
        compiler_params=_params("parallel"),
    )(y, w, res)


def _mlp_kernel(*refs, segments):
    final = segments is not None
    i = pl.program_id(0)
    j = pl.program_id(1)
    if final:
        x_ref, g_ref, w1_ref, w2_ref, fg_ref, o_ref, out_hbm, xn_ref, stage, sems = refs
    else:
        x_hbm, g_ref, w1_ref, w2_ref, o_ref, xn_ref, x_ref, x_sem = refs
        tm = x_ref.shape[0]

        def x_copy(tile):
            rows = pl.ds(pl.multiple_of(tile * tm, tm), tm)
            return pltpu.make_async_copy(x_hbm.at[rows], x_ref, x_sem.at[0])

        @pl.when((i == 0) & (j == 0))
        def _():
            x_copy(0).start()

        @pl.when(j == 0)
        def _():
            x_copy(i).wait()

    @pl.when(j == 0)
    def _():
        x = x_ref[...]
        xn_ref[...] = (x * _rms_scale(x, NORM_EPS) * g_ref[...]).astype(xn_ref.dtype)
        o_ref[...] = x

    if not final:
        @pl.when((j == 1) & (i + 1 < pl.num_programs(0)))
        def _():
            x_copy(i + 1).start()

    h = jnp.dot(xn_ref[...], w1_ref[...].astype(BF16), preferred_element_type=F32)
    h = jnp.square(jnp.maximum(h, 0.0)).astype(BF16)
    o_ref[...] += jnp.dot(h, w2_ref[...].astype(BF16), preferred_element_type=F32)

    if final:
        def copies(tile):
            return [pltpu.make_async_copy(stage.at[pl.ds(row, count)],
                                          out_hbm.at[pl.ds(dst, count)], sems.at[n])
                    for n, (row, count, dst) in enumerate(segments[tile])]

        def per_tile(action):
            for tile in range(len(segments)):
                pl.when(i == tile)(functools.partial(action, tile))

        def wait_previous(tile):
            if tile > 0:
                for cp in copies(tile - 1):
                    cp.wait()

        def start_current(tile):
            for cp in copies(tile):
                cp.start()
            if tile == len(segments) - 1:
                for cp in copies(tile):
                    cp.wait()

        @pl.when(j == pl.num_programs(1) - 1)
        def _():
            per_tile(wait_previous)
            y = o_ref[...]
            stage[...] = y * _rms_scale(y, NORM_EPS) * fg_ref[...]
            per_tile(start_current)


def _output_segments(n_tiles, tm, lp, length, seq):
    table = []
    for tile in range(n_tiles):
        first = tile * tm
        pieces = []
        for b in range(first // lp, (first + tm - 1) // lp + 1):
            lo = max(first, b * lp + N_META_TOKENS)
            hi = min(first + tm, b * lp + length)
            if hi > lo:
                pieces.append((lo - first, hi - lo, b * seq + lo - b * lp - N_META_TOKENS))
        table.append(tuple(pieces))
    return tuple(table)


def _mlp(x, g, w1, w2, layer, final=None):
    t, d = x.shape
    f = w1.shape[2]
    tf = FF_TILE
    tm = MLP_TILE if final is None else TOKEN_TILE
    assert t % tm == 0 and f // tf >= 2
    in_specs = [
        pl.BlockSpec(memory_space=pl.ANY) if final is None
        else pl.BlockSpec((tm, d), lambda i, j: (i, 0)),
        pl.BlockSpec((1, d), lambda i, j: (0, 0)),
        pl.BlockSpec((None, d, tf), lambda i, j: (layer, 0, j)),
        pl.BlockSpec((None, tf, d), lambda i, j: (layer, j, 0)),
    ]
    args = [x, g.reshape(1, d), w1, w2]
    out_specs = pl.BlockSpec((tm, d), lambda i, j: (i, 0))
    out_shape = jax.ShapeDtypeStruct((t, d), F32)
    scratch_shapes = [pltpu.VMEM((tm, d), BF16)]
    if final is None:
        return pl.pallas_call(
            functools.partial(_mlp_kernel, segments=None),
            grid=(t // tm, f // tf),
            in_specs=in_specs,
            out_specs=out_specs,
            out_shape=out_shape,
            scratch_shapes=scratch_shapes + [pltpu.VMEM((tm, d), F32),
                                             pltpu.SemaphoreType.DMA((1,))],
            compiler_params=_params("arbitrary", "arbitrary"),
        )(*args)
    final_g, batch, lp, length = final
    seq = length - N_META_TOKENS
    segments = _output_segments(t // tm, tm, lp, length, seq)
    assert all(row % SUBLANES == 0 and count % SUBLANES == 0 and dst % SUBLANES == 0
               for pieces in segments for row, count, dst in pieces)
    max_pieces = max(len(pieces) for pieces in segments)
    _, out = pl.pallas_call(
        functools.partial(_mlp_kernel, segments=segments),
        grid=(t // tm, f // tf),
        in_specs=in_specs + [pl.BlockSpec((1, d), lambda i, j: (0, 0))],
        out_specs=[out_specs, pl.BlockSpec(memory_space=pl.ANY)],
        out_shape=[out_shape, jax.ShapeDtypeStruct((batch * seq, d), F32)],
        scratch_shapes=scratch_shapes + [pltpu.VMEM((tm, d), F32),
                                         pltpu.SemaphoreType.DMA((max_pieces,))],
        compiler_params=_params("arbitrary", "arbitrary"),
    )(*args, final_g.reshape(1, d))
    return out


def _rglru_kernel(xr_ref, gate_ref, cw_ref, cb_ref, w_ref, b_ref, lam_ref, y_ref,
                  xl, xc_scr, hl, aend, bend, cin, hcar, *, ts):
    c = xr_ref.shape[2]
    n_lane_tiles = c // LANES
    lane_tiles = range(n_lane_tiles)
    seg = SCAN_ROWS // SUBLANES
    n_groups = ts // SCAN_ROWS
    hdr = SUBLANES

    @pl.when(pl.program_id(2) == 0)
    def _():
        xl[:, 0:hdr, :] = jnp.zeros((n_lane_tiles, hdr, LANES), F32)
        hcar[...] = jnp.zeros(hcar.shape, F32)

    for j in lane_tiles:
        xl[j, hdr:hdr + ts, :] = xr_ref[0, :, j * LANES:(j + 1) * LANES]

    def lane(x, j):
        return x[:, j * LANES:(j + 1) * LANES]

    taps = [cw_ref[tap] for tap in range(CONV_WIDTH)]
    bias = cb_ref[...]
    for g in range(n_groups):
        loads = {}
        for k in range(-(CONV_WIDTH - 1), seg):
            loads[k] = xl[:, pl.ds(hdr + g * SCAN_ROWS + k, SUBLANES, stride=seg), :]
        for k in range(seg):
            xc = loads[k] * taps[CONV_WIDTH - 1] + bias
            for back in range(1, CONV_WIDTH):
                xc = xc + loads[k - back] * taps[CONV_WIDTH - 1 - back]
            r0 = g * SCAN_ROWS + k * SUBLANES
            for j in lane_tiles:
                xc_scr[r0:r0 + SUBLANES, j * LANES:(j + 1) * LANES] = xc[j]
    xl[:, 0:hdr, :] = xl[:, ts:ts + hdr, :]

    xc = xc_scr[...]
    half_gates = jnp.dot(xc.astype(BF16), w_ref[0], preferred_element_type=F32) + b_ref[0]
    t_r = jnp.tanh(half_gates[:, :c])
    t_i = jnp.tanh(half_gates[:, c:])
    half_rate = (-0.5 * LRU_C * math.log2(math.e)) * jax.nn.softplus(-lam_ref[...])
    a = jnp.exp2(t_r * half_rate + half_rate)
    bv = jnp.exp(0.5 * jnp.log(1.0 - a * a)) * ((0.5 * t_i + 0.5) * xc)

    def rows(x, g, k):
        r0 = g * SCAN_ROWS + k * SUBLANES
        return x[r0:r0 + SUBLANES, :]

    for g in range(n_groups):
        a_cum, b_cum = rows(a, g, 0), rows(bv, g, 0)
        for k in range(1, seg):
            a_k = rows(a, g, k)
            b_cum = a_k * b_cum + rows(bv, g, k)
            a_cum = a_k * a_cum
        aend[g * SUBLANES:(g + 1) * SUBLANES, :] = a_cum
        bend[g * SUBLANES:(g + 1) * SUBLANES, :] = b_cum

    carry = hcar[0:1, :]
    for s in range(n_groups * SUBLANES):
        cin[s:s + 1, :] = carry
        carry = aend[s:s + 1, :] * carry + bend[s:s + 1, :]
    hcar[0:1, :] = carry

    for g in range(n_groups):
        h = cin[g * SUBLANES:(g + 1) * SUBLANES, :]
        for k in range(seg):
            h = rows(a, g, k) * h + rows(bv, g, k)
            for j in lane_tiles:
                hl[j, pl.ds(g * SCAN_ROWS + k, SUBLANES, stride=seg), :] = lane(h, j)
    for j in lane_tiles:
        lanes = slice(j * LANES, (j + 1) * LANES)
        y_ref[0, :, lanes] = (hl[j] * gate_ref[0, :, lanes].astype(F32)).astype(y_ref.dtype)


def _rglru(xr, gate, conv_w, conv_b, w_gates, b_gates, lam):
    b, l, c = xr.shape
    groups = w_gates.shape[0]
    cg = c // groups
    nl = cg // LANES
    ts = RGLRU_TILE
    assert l % ts == 0
    n_runs = ts // SCAN_ROWS * SUBLANES
    seq_spec = pl.BlockSpec((1, ts, cg), lambda bi, gi, ci: (bi, ci, gi))
    return pl.pallas_call(
        functools.partial(_rglru_kernel, ts=ts),
        grid=(b, groups, l // ts),
        in_specs=[
            seq_spec, seq_spec,
            pl.BlockSpec((CONV_WIDTH, nl, 1, LANES), lambda bi, gi, ci: (0, gi, 0, 0)),
            pl.BlockSpec((nl, 1, LANES), lambda bi, gi, ci: (gi, 0, 0)),
            pl.BlockSpec((1, cg, 2 * cg), lambda bi, gi, ci: (gi, 0, 0)),
            pl.BlockSpec((1, 1, 2 * cg), lambda bi, gi, ci: (gi, 0, 0)),
            pl.BlockSpec((1, cg), lambda bi, gi, ci: (0, gi)),
        ],
        out_specs=seq_spec,
        out_shape=jax.ShapeDtypeStruct((b, l, c), BF16),
        scratch_shapes=[
            pltpu.VMEM((nl, SUBLANES + ts, LANES), F32),
            pltpu.VMEM((ts, cg), F32),
            pltpu.VMEM((nl, ts, LANES), F32),
            pltpu.VMEM((n_runs, cg), F32),
            pltpu.VMEM((n_runs, cg), F32),
            pltpu.VMEM((n_runs, cg), F32),
            pltpu.VMEM((SUBLANES, cg), F32),
        ],
        compiler_params=_params("parallel", "parallel", "arbitrary"),
    )(xr, gate, conv_w.reshape(CONV_WIDTH, c // LANES, 1, LANES),
      conv_b.reshape(c // LANES, 1, LANES), w_gates, b_gates, lam.reshape(1, c))


def _gate_weights(w_r, b_r, w_i, b_i):
    nb, gb, _ = w_r.shape
    groups = nb // GATE_GROUP
    def dense(w):
        w = (0.5 * w).astype(BF16).reshape(groups, GATE_GROUP, gb, gb)
        cols = [jnp.where((jnp.arange(GATE_GROUP) == k)[None, :, None, None], w, 0.0)
                .reshape(groups, GATE_GROUP * gb, gb) for k in range(GATE_GROUP)]
        return jnp.concatenate(cols, axis=2)

    w = jnp.concatenate([dense(w_r), dense(w_i)], axis=2)
    cg = GATE_GROUP * gb
    b = 0.5 * jnp.concatenate([b_r.reshape(groups, 1, cg), b_i.reshape(groups, 1, cg)], axis=2)
    return w, b.astype(F32)


def _attn_kernel(lam_ref, g_ref, q_ref, k_ref, v_ref, o_ref, acc, *, tq, lambda_init):
    qi = pl.program_id(2)
    hw = 2 * HEAD_DIM
    n_chains = acc.shape[0]
    chains = range(n_chains)
    q = q_ref[0]
    q_parts = [q[:, c * HEAD_DIM:(c + 1) * HEAD_DIM] for c in chains]
    acc[...] = jnp.zeros(acc.shape, F32)

    def step(first_block, n_blocks, carry, masked):
        width = n_blocks * tq
        start = pl.multiple_of(first_block * tq, tq)
        k = k_ref[0, pl.ds(start, width), :]
        v = v_ref[0, pl.ds(start, width), :]
        s = [lax.dot_general(q_parts[c], k[:, c * HEAD_DIM:(c + 1) * HEAD_DIM],
                             (((1,), (1,)), ((), ())), preferred_element_type=F32)
             for c in chains]
        if masked:
            row = lax.broadcasted_iota(jnp.int32, s[0].shape, 0)
            col = lax.broadcasted_iota(jnp.int32, s[0].shape, 1)
            keep = col - (n_blocks - 1) * tq <= row
            s = [jnp.where(keep, s[c], -jnp.inf) for c in chains]
        m_new = [jnp.maximum(carry[2 * c], jnp.max(s[c], axis=-1, keepdims=True)) for c in chains]
        p = [jnp.exp2(s[c] - m_new[c]) for c in chains]
        alpha = [jnp.exp2(carry[2 * c] - m_new[c]) for c in chains]
        l_new = [alpha[c] * carry[2 * c + 1] + jnp.sum(p[c], axis=-1, keepdims=True)
                 for c in chains]
        pv = [jnp.dot(p[c].astype(BF16), v[:, (c // 2) * hw:(c // 2 + 1) * hw],
                      preferred_element_type=F32) for c in chains]
        for c in chains:
            acc[c] = alpha[c] * acc[c] + pv[c]
        out = []
        for c in chains:
            out += [m_new[c], l_new[c]]
        return tuple(out)

    neg = jnp.full((tq, 1), -jnp.inf, F32)
    zero = jnp.zeros((tq, 1), F32)
    n_wide = qi // KV_BLOCKS_PER_STEP
    def wide(block, cr):
        return step(block, KV_BLOCKS_PER_STEP, cr, False)

    carry = lax.fori_loop(
        0, n_wide // 2,
        lambda i, cr: wide((2 * i + 1) * KV_BLOCKS_PER_STEP, wide(2 * i * KV_BLOCKS_PER_STEP, cr)),
        (neg, zero) * n_chains)
    carry = lax.cond(n_wide % 2 == 1,
                     lambda cr: wide((n_wide - 1) * KV_BLOCKS_PER_STEP, cr),
                     lambda cr: cr, carry)
    rest = qi - n_wide * KV_BLOCKS_PER_STEP
    tails = [functools.partial(step, qi - r, r + 1, masked=True)
             for r in range(KV_BLOCKS_PER_STEP)]
    carry = lax.switch(rest, tails, carry)

    lv = lam_ref[0]
    lam = (jnp.exp(jnp.sum(lv[0:1] * lv[1:2], axis=-1, keepdims=True))
           - jnp.exp(jnp.sum(lv[2:3] * lv[3:4], axis=-1, keepdims=True)) + lambda_init)
    for head in range(n_chains // 2):
        c0, c1 = 2 * head, 2 * head + 1
        o = acc[c0] * (1.0 / carry[2 * c0 + 1]) - acc[c1] * (lam / carry[2 * c1 + 1])
        o = o * _rms_scale(o, SUBLN_EPS) * g_ref[...] * (1.0 - lambda_init)
        o_ref[0, :, head * hw:(head + 1) * hw] = o.astype(o_ref.dtype)


def _diff_attention(q, k, v, lam_vecs, subln_g, lambda_init):
    b, l, w = q.shape
    hw = 2 * HEAD_DIM
    bw = HEADS_PER_STEP * hw
    n_chains = 2 * HEADS_PER_STEP
    tq = SEQ_TILE
    kv_spec = pl.BlockSpec((1, l, bw), lambda bi, hi, qi: (bi, 0, hi))
    q_spec = pl.BlockSpec((1, tq, bw), lambda bi, hi, qi: (bi, qi, hi))
    return pl.pallas_call(
        functools.partial(_attn_kernel, tq=tq, lambda_init=lambda_init),
        grid=(b, w // bw, l // tq),
        in_specs=[
            pl.BlockSpec((1, 4, HEAD_DIM), lambda bi, hi, qi: (0, 0, 0)),
            pl.BlockSpec((1, hw), lambda bi, hi, qi: (0, 0)),
            q_spec, kv_spec, kv_spec,
        ],
        out_specs=q_spec,
        out_shape=jax.ShapeDtypeStruct((b, l, w), BF16),
        scratch_shapes=[pltpu.VMEM((n_chains, tq, hw), F32)],
        compiler_params=_params("parallel", "parallel", "arbitrary"),
    )(lam_vecs.reshape(1, 4, HEAD_DIM), subln_g.reshape(1, hw), q, k, v)


def _rope_tables(length, batch):
    inv = 1.0 / (ROPE_THETA ** (jnp.arange(0, HEAD_DIM, 2, dtype=F32) / HEAD_DIM))
    ang = jnp.arange(length, dtype=F32)[:, None] * inv[None, :]
    cos, sin = jnp.cos(ang), jnp.sin(ang)
    cos2 = jnp.concatenate([cos, cos], axis=1)
    sin2 = jnp.concatenate([-sin, sin], axis=1)
    return jnp.tile(cos2, (batch, 1)), jnp.tile(sin2, (batch, 1))


def kernel(x, meta_tokens, a_norm_g, a_w_in, a_conv_w, a_conv_b, a_w_r, a_b_r, a_w_i, a_b_i,
           a_lambda, a_w_out, kv_norm_g, w_kv, b_norm_g, b_w_q, b_lambda, b_subln_g, b_w_o,
           mlp_norm_g, mlp_w1, mlp_w2, final_norm_g):
    batch, seq, d = x.shape
    n_a = a_w_in.shape[0]
    depth = mlp_w1.shape[0]
    d_rnn = a_w_out.shape[1]
    qk_width = b_w_q.shape[2]
    v_width = w_kv.shape[1] - qk_width
    length = N_META_TOKENS + seq
    lp = pl.cdiv(length, SEQ_TILE) * SEQ_TILE
    t = batch * lp
    assert t % TOKEN_TILE == 0

    meta = jnp.broadcast_to(meta_tokens[None].astype(x.dtype), (batch, N_META_TOKENS, d))
    pad = jnp.zeros((batch, lp - length, d), x.dtype)
    h = jnp.concatenate([meta, x, pad], axis=1).reshape(t, d)
    rope = _rope_tables(lp, batch)

    w_in, w_out = a_w_in.astype(BF16), a_w_out.astype(BF16)
    wkv, w_q, w_o = w_kv.astype(BF16)[None], b_w_q.astype(BF16), b_w_o.astype(BF16)
    q_scale = math.log2(math.e) / math.sqrt(HEAD_DIM)

    k_sh = v_sh = None
    for layer in range(depth):
        if layer < n_a:
            j = layer
            gate = _norm_matmul(h, a_norm_g[j], w_in, j, 0, d_rnn, BF16, act="gelu")
            xr = _norm_matmul(h, a_norm_g[j], w_in, j, d_rnn, d_rnn, F32)
            w_g, b_g = _gate_weights(a_w_r[j], a_b_r[j], a_w_i[j], a_b_i[j])
            y = _rglru(xr.reshape(batch, lp, d_rnn), gate.reshape(batch, lp, d_rnn),
                       a_conv_w[j], a_conv_b[j], w_g, b_g, a_lambda[j])
            h = _matmul_residual(y.reshape(t, d_rnn), w_out, j, h)
        else:
            j = layer - n_a
            if j == 0:
                k_sh = _norm_matmul(h, kv_norm_g, wkv, 0, 0, qk_width, BF16, rope=rope)
                v_sh = _norm_matmul(h, kv_norm_g, wkv, 0, qk_width, v_width, BF16)
                k_sh = k_sh.reshape(batch, lp, qk_width)
                v_sh = v_sh.reshape(batch, lp, v_width)
            lambda_init = 0.8 - 0.6 * math.exp(-0.3 * layer)
            q = _norm_matmul(h, b_norm_g[j], w_q, j, 0, qk_width, BF16, rope=rope, scale=q_scale)
            o = _diff_attention(q.reshape(batch, lp, qk_width), k_sh, v_sh, b_lambda[j],
                                b_subln_g[j], lambda_init)
            h = _matmul_residual(o.reshape(t, v_width), w_o, j, h)
        final = (final_norm_g, batch, lp, length) if layer == depth - 1 else None
        h = _mlp(h, mlp_norm_g[layer], mlp_w1, mlp_w2, layer, final)
    return h.reshape(batch, seq, d)
```

```python
import functools
import math

import jax
import jax.numpy as jnp
from jax import lax
from jax.experimental import pallas as pl
from jax.experimental.pallas import tpu as pltpu

F32 = jnp.float32
BF16 = jnp.bfloat16

N_META_TOKENS = 16
CONV_WIDTH = 4
LRU_C = 8.0
HEAD_DIM = 128
ROPE_THETA = 10000.0
NORM_EPS = 1e-6
SUBLN_EPS = 1e-5

LANES = 128
SUBLANES = 8
VMEM_LIMIT_BYTES = 56 * 1024 * 1024

SEQ_TILE = 384
TOKEN_TILE = 768
FF_TILE = 512
MLP_TILE = 1056
GATE_GROUP = 4
SCAN_ROWS = 264
RGLRU_TILE = 1056
KV_BLOCKS_PER_STEP = 2
HEADS_PER_STEP = 2


def _params(*semantics):
    return pltpu.CompilerParams(dimension_semantics=semantics,
                                vmem_limit_bytes=VMEM_LIMIT_BYTES)


def _rms_scale(x, eps):
    return lax.rsqrt(jnp.mean(x * x, axis=-1, keepdims=True) + eps)


def _norm_matmul_kernel(*refs, act, rope, scale):
    if rope:
        x_ref, g_ref, w_ref, cos_ref, sin_ref, o_ref = refs
    else:
        x_ref, g_ref, w_ref, o_ref = refs

    x = x_ref[...]
    xn = (x * _rms_scale(x, NORM_EPS) * g_ref[...]).astype(BF16)
    y = jnp.dot(xn, w_ref[...], preferred_element_type=F32)
    if act == "gelu":
        y = jax.nn.gelu(y)
    if rope:
        c = cos_ref[...]
        s = sin_ref[...]
        segs = []
        for i in range(y.shape[1] // HEAD_DIM):
            seg = y[:, i * HEAD_DIM:(i + 1) * HEAD_DIM]
            segs.append(seg * c + pltpu.roll(seg, HEAD_DIM // 2, axis=1) * s)
        y = jnp.concatenate(segs, axis=1)
    if scale != 1.0:
        y = y * scale
    o_ref[...] = y.astype(o_ref.dtype)


def _norm_matmul(x, g, w, layer, col0, n, out_dtype, act=None, rope=None, scale=1.0):
    t, d = x.shape
    tm = TOKEN_TILE
    assert col0 % n == 0
    j0 = col0 // n
    in_specs = [
        pl.BlockSpec((tm, d), lambda i: (i, 0)),
        pl.BlockSpec((1, d), lambda i: (0, 0)),
        pl.BlockSpec((None, d, n), lambda i: (layer, 0, j0), pipeline_mode=pl.Buffered(1)),
    ]
    args = [x, g.reshape(1, d), w]
    if rope is not None:
        in_specs += [pl.BlockSpec((tm, HEAD_DIM), lambda i: (i, 0))] * 2
        args += list(rope)
    return pl.pallas_call(
        functools.partial(_norm_matmul_kernel, act=act, rope=rope is not None, scale=scale),
        grid=(t // tm,),
        in_specs=in_specs,
        out_specs=pl.BlockSpec((tm, n), lambda i: (i, 0)),
        out_shape=jax.ShapeDtypeStruct((t, n), out_dtype),
        compiler_params=_params("parallel"),
    )(*args)


def _matmul_residual_kernel(y_ref, w_ref, r_ref, o_ref):
    o_ref[...] = r_ref[...] + jnp.dot(y_ref[...], w_ref[...], preferred_element_type=F32)


def _matmul_residual(y, w, layer, res):
    t, k = y.shape
    n = w.shape[2]
    tm = TOKEN_TILE
    return pl.pallas_call(
        _matmul_residual_kernel,
        grid=(t // tm,),
        in_specs=[
            pl.BlockSpec((tm, k), lambda i: (i, 0)),
            pl.BlockSpec((None, k, n), lambda i: (layer, 0, 0), pipeline_mode=pl.Buffered(1)),
            pl.BlockSpec((tm, n), lambda i: (i, 0)),
        ],
        out_specs=pl.BlockSpec((tm, n), lambda i: (i, 0)),
        out_shape=jax.ShapeDtypeStruct((t, n), F32),
        compiler_params=_params("parallel"),
    )(y, w, res)


def _mlp_kernel(*refs, segments):
    final = segments is not None
    i = pl.program_id(0)
    j = pl.program_id(1)
    if final:
        (x_hbm, g_ref, w1_ref, w2_ref, fg_ref, out_hbm,
         xn_ref, x_ref, x_sem, o_ref, stage, sems) = refs
    else:
        x_hbm, g_ref, w1_ref, w2_ref, o_ref, xn_ref, x_ref, x_sem = refs
    tm = x_ref.shape[0]

    def x_copy(tile):
        rows = pl.ds(pl.multiple_of(tile * tm, tm), tm)
        return pltpu.make_async_copy(x_hbm.at[rows], x_ref, x_sem.at[0])

    @pl.when((i == 0) & (j == 0))
    def _():
        x_copy(0).start()

    @pl.when(j == 0)
    def _():
        x_copy(i).wait()
        x = x_ref[...]
        xn_ref[...] = (x * _rms_scale(x, NORM_EPS) * g_ref[...]).astype(xn_ref.dtype)
        o_ref[...] = x

    @pl.when((j == 1) & (i + 1 < pl.num_programs(0)))
    def _():
        x_copy(i + 1).start()

    h = jnp.dot(xn_ref[...], w1_ref[...].astype(BF16), preferred_element_type=F32)
    h = jnp.square(jnp.maximum(h, 0.0)).astype(BF16)
    o_ref[...] += jnp.dot(h, w2_ref[...].astype(BF16), preferred_element_type=F32)

    if final:
        def copies(tile):
            return [pltpu.make_async_copy(stage.at[pl.ds(row, count)],
                                          out_hbm.at[pl.ds(dst, count)], sems.at[n])
                    for n, (row, count, dst) in enumerate(segments[tile])]

        def per_tile(action):
            for tile in range(len(segments)):
                pl.when(i == tile)(functools.partial(action, tile))

        def wait_previous(tile):
            if tile > 0:
                for cp in copies(tile - 1):
                    cp.wait()

        def start_current(tile):
            for cp in copies(tile):
                cp.start()
            if tile == len(segments) - 1:
                for cp in copies(tile):
                    cp.wait()

        @pl.when(j == pl.num_programs(1) - 1)
        def _():
            per_tile(wait_previous)
            y = o_ref[...]
            stage[...] = y * _rms_scale(y, NORM_EPS) * fg_ref[...]
            per_tile(start_current)


def _output_segments(n_tiles, tm, lp, length, seq):
    table = []
    for tile in range(n_tiles):
        first = tile * tm
        pieces = []
        for b in range(first // lp, (first + tm - 1) // lp + 1):
            lo = max(first, b * lp + N_META_TOKENS)
            hi = min(first + tm, b * lp + length)
            if hi > lo:
                pieces.append((lo - first, hi - lo, b * seq + lo - b * lp - N_META_TOKENS))
        table.append(tuple(pieces))
    return tuple(table)


def _mlp(x, g, w1, w2, layer, final=None):
    t, d = x.shape
    f = w1.shape[2]
    tm, tf = MLP_TILE, FF_TILE
    assert t % tm == 0 and f // tf >= 2
    grid = (t // tm, f // tf)
    in_specs = [
        pl.BlockSpec(memory_space=pl.ANY),
        pl.BlockSpec((1, d), lambda i, j: (0, 0)),
        pl.BlockSpec((None, d, tf), lambda i, j: (layer, 0, j)),
        pl.BlockSpec((None, tf, d), lambda i, j: (layer, j, 0)),
    ]
    args = [x, g.reshape(1, d), w1, w2]
    scratch_shapes = [pltpu.VMEM((tm, d), BF16), pltpu.VMEM((tm, d), F32),
                      pltpu.SemaphoreType.DMA((1,))]
    params = _params("arbitrary", "arbitrary")
    if final is None:
        return pl.pallas_call(
            functools.partial(_mlp_kernel, segments=None),
            grid=grid,
            in_specs=in_specs,
            out_specs=pl.BlockSpec((tm, d), lambda i, j: (i, 0)),
            out_shape=jax.ShapeDtypeStruct((t, d), F32),
            scratch_shapes=scratch_shapes,
            compiler_params=params,
        )(*args)
    final_g, batch, lp, length = final
    seq = length - N_META_TOKENS
    segments = _output_segments(t // tm, tm, lp, length, seq)
    assert all(row % SUBLANES == 0 and count % SUBLANES == 0 and dst % SUBLANES == 0
               for pieces in segments for row, count, dst in pieces)
    max_pieces = max(len(pieces) for pieces in segments)
    return pl.pallas_call(
        functools.partial(_mlp_kernel, segments=segments),
        grid=grid,
        in_specs=in_specs + [pl.BlockSpec((1, d), lambda i, j: (0, 0))],
        out_specs=pl.BlockSpec(memory_space=pl.ANY),
        out_shape=jax.ShapeDtypeStruct((batch * seq, d), F32),
        scratch_shapes=scratch_shapes + [pltpu.VMEM((tm, d), F32), pltpu.VMEM((tm, d), F32),
                                         pltpu.SemaphoreType.DMA((max_pieces,))],
        compiler_params=params,
    )(*args, final_g.reshape(1, d))


def _rglru_kernel(xr_ref, gate_ref, cw_ref, cb_ref, w_ref, b_ref, lam_ref, y_ref,
                  xl, xc_scr, hl, aend, bend, cin, hcar, *, ts):
    c = xr_ref.shape[2]
    n_lane_tiles = c // LANES
    lane_tiles = range(n_lane_tiles)
    seg = SCAN_ROWS // SUBLANES
    n_groups = ts // SCAN_ROWS
    hdr = SUBLANES

    @pl.when(pl.program_id(2) == 0)
    def _():
        xl[:, 0:hdr, :] = jnp.zeros((n_lane_tiles, hdr, LANES), F32)
        hcar[...] = jnp.zeros(hcar.shape, F32)

    for j in lane_tiles:
        xl[j, hdr:hdr + ts, :] = xr_ref[0, :, j * LANES:(j + 1) * LANES]

    def lane(x, j):
        return x[:, j * LANES:(j + 1) * LANES]

    taps = [cw_ref[tap] for tap in range(CONV_WIDTH)]
    bias = cb_ref[...]
    for g in range(n_groups):
        loads = {}
        for k in range(-(CONV_WIDTH - 1), seg):
            loads[k] = xl[:, pl.ds(hdr + g * SCAN_ROWS + k, SUBLANES, stride=seg), :]
        for k in range(seg):
            xc = loads[k] * taps[CONV_WIDTH - 1] + bias
            for back in range(1, CONV_WIDTH):
                xc = xc + loads[k - back] * taps[CONV_WIDTH - 1 - back]
            r0 = g * SCAN_ROWS + k * SUBLANES
            for j in lane_tiles:
                xc_scr[r0:r0 + SUBLANES, j * LANES:(j + 1) * LANES] = xc[j]
    xl[:, 0:hdr, :] = xl[:, ts:ts + hdr, :]

    xc = xc_scr[...]
    half_gates = jnp.dot(xc.astype(BF16), w_ref[0], preferred_element_type=F32) + b_ref[0]
    t_r = jnp.tanh(half_gates[:, :c])
    t_i = jnp.tanh(half_gates[:, c:])
    half_rate = (-0.5 * LRU_C * math.log2(math.e)) * jax.nn.softplus(-lam_ref[...])
    a = jnp.exp2(t_r * half_rate + half_rate)
    bv = jnp.exp(0.5 * jnp.log(1.0 - a * a)) * ((0.5 * t_i + 0.5) * xc)

    def rows(x, g, k):
        r0 = g * SCAN_ROWS + k * SUBLANES
        return x[r0:r0 + SUBLANES, :]

    for g in range(n_groups):
        a_cum, b_cum = rows(a, g, 0), rows(bv, g, 0)
        for k in range(1, seg):
            a_k = rows(a, g, k)
            b_cum = a_k * b_cum + rows(bv, g, k)
            a_cum = a_k * a_cum
        aend[g * SUBLANES:(g + 1) * SUBLANES, :] = a_cum
        bend[g * SUBLANES:(g + 1) * SUBLANES, :] = b_cum

    carry = hcar[0:1, :]
    for s in range(n_groups * SUBLANES):
        cin[s:s + 1, :] = carry
        carry = aend[s:s + 1, :] * carry + bend[s:s + 1, :]
    hcar[0:1, :] = carry

    for g in range(n_groups):
        h = cin[g * SUBLANES:(g + 1) * SUBLANES, :]
        for k in range(seg):
            h = rows(a, g, k) * h + rows(bv, g, k)
            for j in lane_tiles:
                hl[j, pl.ds(g * SCAN_ROWS + k, SUBLANES, stride=seg), :] = lane(h, j)
    for j in lane_tiles:
        lanes = slice(j * LANES, (j + 1) * LANES)
        y_ref[0, :, lanes] = (hl[j] * gate_ref[0, :, lanes].astype(F32)).astype(y_ref.dtype)


def _rglru(xr, gate, conv_w, conv_b, w_gates, b_gates, lam):
    b, l, c = xr.shape
    groups = w_gates.shape[0]
    cg = c // groups
    nl = cg // LANES
    ts = RGLRU_TILE
    assert l % ts == 0
    n_runs = ts // SCAN_ROWS * SUBLANES
    seq_spec = pl.BlockSpec((1, ts, cg), lambda bi, gi, ci: (bi, ci, gi))
    return pl.pallas_call(
        functools.partial(_rglru_kernel, ts=ts),
        grid=(b, groups, l // ts),
        in_specs=[
            seq_spec, seq_spec,
            pl.BlockSpec((CONV_WIDTH, nl, 1, LANES), lambda bi, gi, ci: (0, gi, 0, 0)),
            pl.BlockSpec((nl, 1, LANES), lambda bi, gi, ci: (gi, 0, 0)),
            pl.BlockSpec((1, cg, 2 * cg), lambda bi, gi, ci: (gi, 0, 0)),
            pl.BlockSpec((1, 1, 2 * cg), lambda bi, gi, ci: (gi, 0, 0)),
            pl.BlockSpec((1, cg), lambda bi, gi, ci: (0, gi)),
        ],
        out_specs=seq_spec,
        out_shape=jax.ShapeDtypeStruct((b, l, c), BF16),
        scratch_shapes=[
            pltpu.VMEM((nl, SUBLANES + ts, LANES), F32),
            pltpu.VMEM((ts, cg), F32),
            pltpu.VMEM((nl, ts, LANES), F32),
            pltpu.VMEM((n_runs, cg), F32),
            pltpu.VMEM((n_runs, cg), F32),
            pltpu.VMEM((n_runs, cg), F32),
            pltpu.VMEM((SUBLANES, cg), F32),
        ],
        compiler_params=_params("parallel", "parallel", "arbitrary"),
    )(xr, gate, conv_w.reshape(CONV_WIDTH, c // LANES, 1, LANES),
      conv_b.reshape(c // LANES, 1, LANES), w_gates, b_gates, lam.reshape(1, c))


def _gate_weights(w_r, b_r, w_i, b_i):
    nb, gb, _ = w_r.shape
    groups = nb // GATE_GROUP
    def dense(w):
        w = (0.5 * w).astype(BF16).reshape(groups, GATE_GROUP, gb, gb)
        cols = [jnp.where((jnp.arange(GATE_GROUP) == k)[None, :, None, None], w, 0.0)
                .reshape(groups, GATE_GROUP * gb, gb) for k in range(GATE_GROUP)]
        return jnp.concatenate(cols, axis=2)

    w = jnp.concatenate([dense(w_r), dense(w_i)], axis=2)
    cg = GATE_GROUP * gb
    b = 0.5 * jnp.concatenate([b_r.reshape(groups, 1, cg), b_i.reshape(groups, 1, cg)], axis=2)
    return w, b.astype(F32)


def _attn_kernel(lam_ref, g_ref, q_ref, k_ref, v_ref, o_ref, acc, *, tq, lambda_init):
    qi = pl.program_id(2)
    hw = 2 * HEAD_DIM
    n_chains = acc.shape[0]
    chains = range(n_chains)
    q = q_ref[0]
    q_parts = [q[:, c * HEAD_DIM:(c + 1) * HEAD_DIM] for c in chains]
    acc[...] = jnp.zeros(acc.shape, F32)

    def step(first_block, n_blocks, carry, masked):
        width = n_blocks * tq
        start = pl.multiple_of(first_block * tq, tq)
        k = k_ref[0, pl.ds(start, width), :]
        v = v_ref[0, pl.ds(start, width), :]
        s = [lax.dot_general(q_parts[c], k[:, c * HEAD_DIM:(c + 1) * HEAD_DIM],
                             (((1,), (1,)), ((), ())), preferred_element_type=F32)
             for c in chains]
        if masked:
            row = lax.broadcasted_iota(jnp.int32, s[0].shape, 0)
            col = lax.broadcasted_iota(jnp.int32, s[0].shape, 1)
            keep = col - (n_blocks - 1) * tq <= row
            s = [jnp.where(keep, s[c], -jnp.inf) for c in chains]
        m_new = [jnp.maximum(carry[2 * c], jnp.max(s[c], axis=-1, keepdims=True)) for c in chains]
        p = [jnp.exp2(s[c] - m_new[c]) for c in chains]
        alpha = [jnp.exp2(carry[2 * c] - m_new[c]) for c in chains]
        l_new = [alpha[c] * carry[2 * c + 1] + jnp.sum(p[c], axis=-1, keepdims=True)
                 for c in chains]
        pv = [jnp.dot(p[c].astype(BF16), v[:, (c // 2) * hw:(c // 2 + 1) * hw],
                      preferred_element_type=F32) for c in chains]
        for c in chains:
            acc[c] = alpha[c] * acc[c] + pv[c]
        out = []
        for c in chains:
            out += [m_new[c], l_new[c]]
        return tuple(out)

    neg = jnp.full((tq, 1), -jnp.inf, F32)
    zero = jnp.zeros((tq, 1), F32)
    n_wide = qi // KV_BLOCKS_PER_STEP
    def wide(block, cr):
        return step(block, KV_BLOCKS_PER_STEP, cr, False)

    carry = lax.fori_loop(
        0, n_wide // 2,
        lambda i, cr: wide((2 * i + 1) * KV_BLOCKS_PER_STEP, wide(2 * i * KV_BLOCKS_PER_STEP, cr)),
        (neg, zero) * n_chains)
    carry = lax.cond(n_wide % 2 == 1,
                     lambda cr: wide((n_wide - 1) * KV_BLOCKS_PER_STEP, cr),
                     lambda cr: cr, carry)
    rest = qi - n_wide * KV_BLOCKS_PER_STEP
    tails = [functools.partial(step, qi - r, r + 1, masked=True)
             for r in range(KV_BLOCKS_PER_STEP)]
    carry = lax.switch(rest, tails, carry)

    lv = lam_ref[0]
    lam = (jnp.exp(jnp.sum(lv[0:1] * lv[1:2], axis=-1, keepdims=True))
           - jnp.exp(jnp.sum(lv[2:3] * lv[3:4], axis=-1, keepdims=True)) + lambda_init)
    for head in range(n_chains // 2):
        c0, c1 = 2 * head, 2 * head + 1
        o = acc[c0] * (1.0 / carry[2 * c0 + 1]) - acc[c1] * (lam / carry[2 * c1 + 1])
        o = o * _rms_scale(o, SUBLN_EPS) * g_ref[...] * (1.0 - lambda_init)
        o_ref[0, :, head * hw:(head + 1) * hw] = o.astype(o_ref.dtype)


def _diff_attention(q, k, v, lam_vecs, subln_g, lambda_init):
    b, l, w = q.shape
    hw = 2 * HEAD_DIM
    bw = HEADS_PER_STEP * hw
    n_chains = 2 * HEADS_PER_STEP
    tq = SEQ_TILE
    kv_spec = pl.BlockSpec((1, l, bw), lambda bi, hi, qi: (bi, 0, hi))
    q_spec = pl.BlockSpec((1, tq, bw), lambda bi, hi, qi: (bi, qi, hi))
    return pl.pallas_call(
        functools.partial(_attn_kernel, tq=tq, lambda_init=lambda_init),
        grid=(b, w // bw, l // tq),
        in_specs=[
            pl.BlockSpec((1, 4, HEAD_DIM), lambda bi, hi, qi: (0, 0, 0)),
            pl.BlockSpec((1, hw), lambda bi, hi, qi: (0, 0)),
            q_spec, kv_spec, kv_spec,
        ],
        out_specs=q_spec,
        out_shape=jax.ShapeDtypeStruct((b, l, w), BF16),
        scratch_shapes=[pltpu.VMEM((n_chains, tq, hw), F32)],
        compiler_params=_params("parallel", "parallel", "arbitrary"),
    )(lam_vecs.reshape(1, 4, HEAD_DIM), subln_g.reshape(1, hw), q, k, v)


def _rope_tables(length, batch):
    inv = 1.0 / (ROPE_THETA ** (jnp.arange(0, HEAD_DIM, 2, dtype=F32) / HEAD_DIM))
    ang = jnp.arange(length, dtype=F32)[:, None] * inv[None, :]
    cos, sin = jnp.cos(ang), jnp.sin(ang)
    cos2 = jnp.concatenate([cos, cos], axis=1)
    sin2 = jnp.concatenate([-sin, sin], axis=1)
    return jnp.tile(cos2, (batch, 1)), jnp.tile(sin2, (batch, 1))


def kernel(x, meta_tokens, a_norm_g, a_w_in, a_conv_w, a_conv_b, a_w_r, a_b_r, a_w_i, a_b_i,
           a_lambda, a_w_out, kv_norm_g, w_kv, b_norm_g, b_w_q, b_lambda, b_subln_g, b_w_o,
           mlp_norm_g, mlp_w1, mlp_w2, final_norm_g):
    batch, seq, d = x.shape
    n_a = a_w_in.shape[0]
    depth = mlp_w1.shape[0]
    d_rnn = a_w_out.shape[1]
    qk_width = b_w_q.shape[2]
    v_width = w_kv.shape[1] - qk_width
    length = N_META_TOKENS + seq
    lp = pl.cdiv(length, SEQ_TILE) * SEQ_TILE
    t = batch * lp
    assert t % TOKEN_TILE == 0

    meta = jnp.broadcast_to(meta_tokens[None].astype(x.dtype), (batch, N_META_TOKENS, d))
    pad = jnp.zeros((batch, lp - length, d), x.dtype)
    h = jnp.concatenate([meta, x, pad], axis=1).reshape(t, d)
    rope = _rope_tables(lp, batch)

    w_in, w_out = a_w_in.astype(BF16), a_w_out.astype(BF16)
    wkv, w_q, w_o = w_kv.astype(BF16)[None], b_w_q.astype(BF16), b_w_o.astype(BF16)
    q_scale = math.log2(math.e) / math.sqrt(HEAD_DIM)

    k_sh = v_sh = None
    for layer in range(depth):
        if layer < n_a:
            j = layer
            gate = _norm_matmul(h, a_norm_g[j], w_in, j, 0, d_rnn, BF16, act="gelu")
            xr = _norm_matmul(h, a_norm_g[j], w_in, j, d_rnn, d_rnn, F32)
            w_g, b_g = _gate_weights(a_w_r[j], a_b_r[j], a_w_i[j], a_b_i[j])
            y = _rglru(xr.reshape(batch, lp, d_rnn), gate.reshape(batch, lp, d_rnn),
                       a_conv_w[j], a_conv_b[j], w_g, b_g, a_lambda[j])
            h = _matmul_residual(y.reshape(t, d_rnn), w_out, j, h)
        else:
            j = layer - n_a
            if j == 0:
                k_sh = _norm_matmul(h, kv_norm_g, wkv, 0, 0, qk_width, BF16, rope=rope)
                v_sh = _norm_matmul(h, kv_norm_g, wkv, 0, qk_width, v_width, BF16)
                k_sh = k_sh.reshape(batch, lp, qk_width)
                v_sh = v_sh.reshape(batch, lp, v_width)
            lambda_init = 0.8 - 0.6 * math.exp(-0.3 * layer)
            q = _norm_matmul(h, b_norm_g[j], w_q, j, 0, qk_width, BF16, rope=rope, scale=q_scale)
            o = _diff_attention(q.reshape(batch, lp, qk_width), k_sh, v_sh, b_lambda[j],
                                b_subln_g[j], lambda_init)
            h = _matmul_residual(o.reshape(t, v_width), w_o, j, h)
        final = (final_norm_g, batch, lp, length) if layer == depth - 1 else None
        h = _mlp(h, mlp_norm_g[layer], mlp_w1, mlp_w2, layer, final)
    return h.reshape(batch, seq, d)
```

```python
import functools
import math

import jax
import jax.numpy as jnp
from jax import lax
from jax.experimental import pallas as pl
from jax.experimental.pallas import tpu as pltpu

F32 = jnp.float32
BF16 = jnp.bfloat16

N_META_TOKENS = 16
CONV_WIDTH = 4
LRU_C = 8.0
HEAD_DIM = 128
ROPE_THETA = 10000.0
NORM_EPS = 1e-6
SUBLN_EPS = 1e-5

LANES = 128
SUBLANES = 8
VMEM_LIMIT_BYTES = 56 * 1024 * 1024

SEQ_TILE = 384
TOKEN_TILE = 768
FF_TILE = 512
MLP_TILE = 1056
GATE_GROUP = 4
SCAN_ROWS = 264
RGLRU_TILE = 1056
KV_BLOCKS_PER_STEP = 2
HEADS_PER_STEP = 2


def _params(*semantics):
    return pltpu.CompilerParams(dimension_semantics=semantics,
                                vmem_limit_bytes=VMEM_LIMIT_BYTES)


def _rms_scale(x, eps):
    return lax.rsqrt(jnp.mean(x * x, axis=-1, keepdims=True) + eps)


def _norm_matmul_kernel(*refs, act, rope, scale):
    if rope:
        x_ref, g_ref, w_ref, cos_ref, sin_ref, o_ref = refs
    else:
        x_ref, g_ref, w_ref, o_ref = refs

    x = x_ref[...]
    xn = (x * _rms_scale(x, NORM_EPS) * g_ref[...]).astype(BF16)
    y = jnp.dot(xn, w_ref[...], preferred_element_type=F32)
    if act == "gelu":
        y = jax.nn.gelu(y)
    if rope:
        c = cos_ref[...]
        s = sin_ref[...]
        segs = []
        for i in range(y.shape[1] // HEAD_DIM):
            seg = y[:, i * HEAD_DIM:(i + 1) * HEAD_DIM]
            segs.append(seg * c + pltpu.roll(seg, HEAD_DIM // 2, axis=1) * s)
        y = jnp.concatenate(segs, axis=1)
    if scale != 1.0:
        y = y * scale
    o_ref[...] = y.astype(o_ref.dtype)


def _norm_matmul(x, g, w, layer, col0, n, out_dtype, act=None, rope=None, scale=1.0):
    t, d = x.shape
    tm = TOKEN_TILE
    assert col0 % n == 0
    j0 = col0 // n
    in_specs = [
        pl.BlockSpec((tm, d), lambda i: (i, 0)),
        pl.BlockSpec((1, d), lambda i: (0, 0)),
        pl.BlockSpec((None, d, n), lambda i: (layer, 0, j0), pipeline_mode=pl.Buffered(1)),
    ]
    args = [x, g.reshape(1, d), w]
    if rope is not None:
        in_specs += [pl.BlockSpec((tm, HEAD_DIM), lambda i: (i, 0))] * 2
        args += list(rope)
    return pl.pallas_call(
        functools.partial(_norm_matmul_kernel, act=act, rope=rope is not None, scale=scale),
        grid=(t // tm,),
        in_specs=in_specs,
        out_specs=pl.BlockSpec((tm, n), lambda i: (i, 0)),
        out_shape=jax.ShapeDtypeStruct((t, n), out_dtype),
        compiler_params=_params("parallel"),
    )(*args)


def _matmul_residual_kernel(y_ref, w_ref, r_ref, o_ref):
    o_ref[...] = r_ref[...] + jnp.dot(y_ref[...], w_ref[...], preferred_element_type=F32)


def _matmul_residual(y, w, layer, res):
    t, k = y.shape
    n = w.shape[2]
    tm = TOKEN_TILE
    return pl.pallas_call(
        _matmul_residual_kernel,
        grid=(t // tm,),
        in_specs=[
            pl.BlockSpec((tm, k), lambda i: (i, 0)),
            pl.BlockSpec((None, k, n), lambda i: (layer, 0, 0), pipeline_mode=pl.Buffered(1)),
            pl.BlockSpec((tm, n), lambda i: (i, 0)),
        ],
        out_specs=pl.BlockSpec((tm, n), lambda i: (i, 0)),
        out_shape=jax.ShapeDtypeStruct((t, n), F32),
        compiler_params=_params("parallel"),
    )(y, w, res)


def _mlp_kernel(*refs, segments):
    final = segments is not None
    i = pl.program_id(0)
    j = pl.program_id(1)
    if final:
        (x_hbm, g_ref, w1_ref, w2_ref, fg_ref, out_hbm,
         xn_ref, x_ref, x_sem, o_ref, stage, sems) = refs
    else:
        x_hbm, g_ref, w1_ref, w2_ref, o_ref, xn_ref, x_ref, x_sem = refs
    tm = x_ref.shape[0]

    def x_copy(tile):
        rows = pl.ds(pl.multiple_of(tile * tm, tm), tm)
        return pltpu.make_async_copy(x_hbm.at[rows], x_ref, x_sem.at[0])

    @pl.when((i == 0) & (j == 0))
    def _():
        x_copy(0).start()

    @pl.when(j == 0)
    def _():
        x_copy(i).wait()
        x = x_ref[...]
        xn_ref[...] = (x * _rms_scale(x, NORM_EPS) * g_ref[...]).astype(xn_ref.dtype)
        o_ref[...] = x

    @pl.when((j == 1) & (i + 1 < pl.num_programs(0)))
    def _():
        x_copy(i + 1).start()

    h = jnp.dot(xn_ref[...], w1_ref[...].astype(BF16), preferred_element_type=F32)
    h = jnp.square(jnp.maximum(h, 0.0)).astype(BF16)
    o_ref[...] += jnp.dot(h, w2_ref[...].astype(BF16), preferred_element_type=F32)

    if final:
        def copies(tile):
            return [pltpu.make_async_copy(stage.at[pl.ds(row, count)],
                                          out_hbm.at[pl.ds(dst, count)], sems.at[n])
                    for n, (row, count, dst) in enumerate(segments[tile])]

        def per_tile(action):
            for tile in range(len(segments)):
                pl.when(i == tile)(functools.partial(action, tile))

        def wait_previous(tile):
            if tile > 0:
                for cp in copies(tile - 1):
                    cp.wait()

        def start_current(tile):
            for cp in copies(tile):
                cp.start()
            if tile == len(segments) - 1:
                for cp in copies(tile):
                    cp.wait()

        @pl.when(j == pl.num_programs(1) - 1)
        def _():
            per_tile(wait_previous)
            y = o_ref[...]
            stage[...] = y * _rms_scale(y, NORM_EPS) * fg_ref[...]
            per_tile(start_current)


def _output_segments(n_tiles, tm, lp, length, seq):
    table = []
    for tile in range(n_tiles):
        first = tile * tm
        pieces = []
        for b in range(first // lp, (first + tm - 1) // lp + 1):
            lo = max(first, b * lp + N_META_TOKENS)
            hi = min(first + tm, b * lp + length)
            if hi > lo:
                pieces.append((lo - first, hi - lo, b * seq + lo - b * lp - N_META_TOKENS))
        table.append(tuple(pieces))
    return tuple(table)


def _mlp(x, g, w1, w2, layer, final=None):
    t, d = x.shape
    f = w1.shape[2]
    tm, tf = MLP_TILE, FF_TILE
    assert t % tm == 0 and f // tf >= 2
    grid = (t // tm, f // tf)
    in_specs = [
        pl.BlockSpec(memory_space=pl.ANY),
        pl.BlockSpec((1, d), lambda i, j: (0, 0)),
        pl.BlockSpec((None, d, tf), lambda i, j: (layer, 0, j)),
        pl.BlockSpec((None, tf, d), lambda i, j: (layer, j, 0)),
    ]
    args = [x, g.reshape(1, d), w1, w2]
    scratch_shapes = [pltpu.VMEM((tm, d), BF16), pltpu.VMEM((tm, d), F32),
                      pltpu.SemaphoreType.DMA((1,))]
    params = _params("arbitrary", "arbitrary")
    if final is None:
        return pl.pallas_call(
            functools.partial(_mlp_kernel, segments=None),
            grid=grid,
            in_specs=in_specs,
            out_specs=pl.BlockSpec((tm, d), lambda i, j: (i, 0)),
            out_shape=jax.ShapeDtypeStruct((t, d), F32),
            scratch_shapes=scratch_shapes,
            compiler_params=params,
        )(*args)
    final_g, batch, lp, length = final
    seq = length - N_META_TOKENS
    segments = _output_segments(t // tm, tm, lp, length, seq)
    assert all(row % SUBLANES == 0 and count % SUBLANES == 0 and dst % SUBLANES == 0
               for pieces in segments for row, count, dst in pieces)
    max_pieces = max(len(pieces) for pieces in segments)
    return pl.pallas_call(
        functools.partial(_mlp_kernel, segments=segments),
        grid=grid,
        in_specs=in_specs + [pl.BlockSpec((1, d), lambda i, j: (0, 0))],
        out_specs=pl.BlockSpec(memory_space=pl.ANY),
        out_shape=jax.ShapeDtypeStruct((batch * seq, d), F32),
        scratch_shapes=scratch_shapes + [pltpu.VMEM((tm, d), F32), pltpu.VMEM((tm, d), F32),
                                         pltpu.SemaphoreType.DMA((max_pieces,))],
        compiler_params=params,
    )(*args, final_g.reshape(1, d))


def _rglru_kernel(xr_ref, gate_ref, cw_ref, cb_ref, w_ref, b_ref, lam_ref, y_ref,
                  xl, xc_scr, hl, aend, bend, cin, hcar, *, ts):
    c = xr_ref.shape[2]
    n_lane_tiles = c // LANES
    lane_tiles = range(n_lane_tiles)
    seg = SCAN_ROWS // SUBLANES
    n_groups = ts // SCAN_ROWS
    hdr = SUBLANES

    @pl.when(pl.program_id(2) == 0)
    def _():
        xl[:, 0:hdr, :] = jnp.zeros((n_lane_tiles, hdr, LANES), F32)
        hcar[...] = jnp.zeros(hcar.shape, F32)

    for j in lane_tiles:
        xl[j, hdr:hdr + ts, :] = xr_ref[0, :, j * LANES:(j + 1) * LANES]

    def lane(x, j):
        return x[:, j * LANES:(j + 1) * LANES]

    taps = [cw_ref[tap] for tap in range(CONV_WIDTH)]
    bias = cb_ref[...]
    for g in range(n_groups):
        loads = {}
        for k in range(-(CONV_WIDTH - 1), seg):
            loads[k] = xl[:, pl.ds(hdr + g * SCAN_ROWS + k, SUBLANES, stride=seg), :]
        for k in range(seg):
            xc = loads[k] * taps[CONV_WIDTH - 1] + bias
            for back in range(1, CONV_WIDTH):
                xc = xc + loads[k - back] * taps[CONV_WIDTH - 1 - back]
            r0 = g * SCAN_ROWS + k * SUBLANES
            for j in lane_tiles:
                xc_scr[r0:r0 + SUBLANES, j * LANES:(j + 1) * LANES] = xc[j]
    xl[:, 0:hdr, :] = xl[:, ts:ts + hdr, :]

    xc = xc_scr[...]
    half_gates = jnp.dot(xc.astype(BF16), w_ref[0], preferred_element_type=F32) + b_ref[0]
    t_r = jnp.tanh(half_gates[:, :c])
    t_i = jnp.tanh(half_gates[:, c:])
    half_rate = (-0.5 * LRU_C * math.log2(math.e)) * jax.nn.softplus(-lam_ref[...])
    a = jnp.exp2(t_r * half_rate + half_rate)
    bv = jnp.exp(0.5 * jnp.log(1.0 - a * a)) * ((0.5 * t_i + 0.5) * xc)

    def rows(x, g, k):
        r0 = g * SCAN_ROWS + k * SUBLANES
        return x[r0:r0 + SUBLANES, :]

    for g in range(n_groups):
        a_cum, b_cum = rows(a, g, 0), rows(bv, g, 0)
        for k in range(1, seg):
            a_k = rows(a, g, k)
            b_cum = a_k * b_cum + rows(bv, g, k)
            a_cum = a_k * a_cum
        aend[g * SUBLANES:(g + 1) * SUBLANES, :] = a_cum
        bend[g * SUBLANES:(g + 1) * SUBLANES, :] = b_cum

    carry = hcar[0:1, :]
    for s in range(n_groups * SUBLANES):
        cin[s:s + 1, :] = carry
        carry = aend[s:s + 1, :] * carry + bend[s:s + 1, :]
    hcar[0:1, :] = carry

    for g in range(n_groups):
        h = cin[g * SUBLANES:(g + 1) * SUBLANES, :]
        for k in range(seg):
            h = rows(a, g, k) * h + rows(bv, g, k)
            for j in lane_tiles:
                hl[j, pl.ds(g * SCAN_ROWS + k, SUBLANES, stride=seg), :] = lane(h, j)
    for j in lane_tiles:
        lanes = slice(j * LANES, (j + 1) * LANES)
        y_ref[0, :, lanes] = (hl[j] * gate_ref[0, :, lanes].astype(F32)).astype(y_ref.dtype)


def _rglru(xr, gate, conv_w, conv_b, w_gates, b_gates, lam):
    b, l, c = xr.shape
    groups = w_gates.shape[0]
    cg = c // groups
    nl = cg // LANES
    ts = RGLRU_TILE
    assert l % ts == 0
    n_runs = ts // SCAN_ROWS * SUBLANES
    seq_spec = pl.BlockSpec((1, ts, cg), lambda bi, gi, ci: (bi, ci, gi))
    return pl.pallas_call(
        functools.partial(_rglru_kernel, ts=ts),
        grid=(b, groups, l // ts),
        in_specs=[
            seq_spec, seq_spec,
            pl.BlockSpec((CONV_WIDTH, nl, 1, LANES), lambda bi, gi, ci: (0, gi, 0, 0)),
            pl.BlockSpec((nl, 1, LANES), lambda bi, gi, ci: (gi, 0, 0)),
            pl.BlockSpec((1, cg, 2 * cg), lambda bi, gi, ci: (gi, 0, 0)),
            pl.BlockSpec((1, 1, 2 * cg), lambda bi, gi, ci: (gi, 0, 0)),
            pl.BlockSpec((1, cg), lambda bi, gi, ci: (0, gi)),
        ],
        out_specs=seq_spec,
        out_shape=jax.ShapeDtypeStruct((b, l, c), BF16),
        scratch_shapes=[
            pltpu.VMEM((nl, SUBLANES + ts, LANES), F32),
            pltpu.VMEM((ts, cg), F32),
            pltpu.VMEM((nl, ts, LANES), F32),
            pltpu.VMEM((n_runs, cg), F32),
            pltpu.VMEM((n_runs, cg), F32),
            pltpu.VMEM((n_runs, cg), F32),
            pltpu.VMEM((SUBLANES, cg), F32),
        ],
        compiler_params=_params("parallel", "parallel", "arbitrary"),
    )(xr, gate, conv_w.reshape(CONV_WIDTH, c // LANES, 1, LANES),
      conv_b.reshape(c // LANES, 1, LANES), w_gates, b_gates, lam.reshape(1, c))


def _gate_weights(w_r, b_r, w_i, b_i):
    nb, gb, _ = w_r.shape
    groups = nb // GATE_GROUP
    cg = GATE_GROUP * gb
    block_col = jnp.arange(GATE_GROUP)[:, None, None] * gb + jnp.arange(gb)[None, :, None]
    place = (jnp.arange(cg)[None, None, :] == block_col).astype(BF16)

    def dense(w):
        w = (0.5 * w).astype(BF16).reshape(groups, GATE_GROUP, gb, gb)
        return jnp.einsum("gjab,jbc->gjac", w, place,
                          preferred_element_type=BF16).reshape(groups, cg, cg)

    w = jnp.concatenate([dense(w_r), dense(w_i)], axis=2)
    b = 0.5 * jnp.concatenate([b_r.reshape(groups, 1, cg), b_i.reshape(groups, 1, cg)], axis=2)
    return w, b.astype(F32)


def _attn_kernel(lam_ref, g_ref, q_ref, k_ref, v_ref, o_ref, acc, *, tq, lambda_init):
    qi = pl.program_id(2)
    hw = 2 * HEAD_DIM
    n_chains = acc.shape[0]
    chains = range(n_chains)
    q = q_ref[0]
    q_parts = [q[:, c * HEAD_DIM:(c + 1) * HEAD_DIM] for c in chains]
    acc[...] = jnp.zeros(acc.shape, F32)

    def step(first_block, n_blocks, carry, masked):
        width = n_blocks * tq
        start = pl.multiple_of(first_block * tq, tq)
        k = k_ref[0, pl.ds(start, width), :]
        v = v_ref[0, pl.ds(start, width), :]
        s = [lax.dot_general(q_parts[c], k[:, c * HEAD_DIM:(c + 1) * HEAD_DIM],
                             (((1,), (1,)), ((), ())), preferred_element_type=F32)
             for c in chains]
        if masked:
            row = lax.broadcasted_iota(jnp.int32, s[0].shape, 0)
            col = lax.broadcasted_iota(jnp.int32, s[0].shape, 1)
            keep = col - (n_blocks - 1) * tq <= row
            s = [jnp.where(keep, s[c], -jnp.inf) for c in chains]
        m_new = [jnp.maximum(carry[2 * c], jnp.max(s[c], axis=-1, keepdims=True)) for c in chains]
        p = [jnp.exp2(s[c] - m_new[c]) for c in chains]
        alpha = [jnp.exp2(carry[2 * c] - m_new[c]) for c in chains]
        l_new = [alpha[c] * carry[2 * c + 1] + jnp.sum(p[c], axis=-1, keepdims=True)
                 for c in chains]
        pv = [jnp.dot(p[c].astype(BF16), v[:, (c // 2) * hw:(c // 2 + 1) * hw],
                      preferred_element_type=F32) for c in chains]
        for c in chains:
            acc[c] = alpha[c] * acc[c] + pv[c]
        out = []
        for c in chains:
            out += [m_new[c], l_new[c]]
        return tuple(out)

    neg = jnp.full((tq, 1), -jnp.inf, F32)
    zero = jnp.zeros((tq, 1), F32)
    n_wide = qi // KV_BLOCKS_PER_STEP
    def wide(block, cr):
        return step(block, KV_BLOCKS_PER_STEP, cr, False)

    carry = lax.fori_loop(
        0, n_wide // 2,
        lambda i, cr: wide((2 * i + 1) * KV_BLOCKS_PER_STEP, wide(2 * i * KV_BLOCKS_PER_STEP, cr)),
        (neg, zero) * n_chains)
    carry = lax.cond(n_wide % 2 == 1,
                     lambda cr: wide((n_wide - 1) * KV_BLOCKS_PER_STEP, cr),
                     lambda cr: cr, carry)
    rest = qi - n_wide * KV_BLOCKS_PER_STEP
    tails = [functools.partial(step, qi - r, r + 1, masked=True)
             for r in range(KV_BLOCKS_PER_STEP)]
    carry = lax.switch(rest, tails, carry)

    lv = lam_ref[0]
    lam = (jnp.exp(jnp.sum(lv[0:1] * lv[1:2], axis=-1, keepdims=True))
           - jnp.exp(jnp.sum(lv[2:3] * lv[3:4], axis=-1, keepdims=True)) + lambda_init)
    for head in range(n_chains // 2):
        c0, c1 = 2 * head, 2 * head + 1
        o = acc[c0] * (1.0 / carry[2 * c0 + 1]) - acc[c1] * (lam / carry[2 * c1 + 1])
        o = o * _rms_scale(o, SUBLN_EPS) * g_ref[...] * (1.0 - lambda_init)
        o_ref[0, :, head * hw:(head + 1) * hw] = o.astype(o_ref.dtype)


def _diff_attention(q, k, v, lam_vecs, subln_g, lambda_init):
    b, l, w = q.shape
    hw = 2 * HEAD_DIM
    bw = HEADS_PER_STEP * hw
    n_chains = 2 * HEADS_PER_STEP
    tq = SEQ_TILE
    kv_spec = pl.BlockSpec((1, l, bw), lambda bi, hi, qi: (bi, 0, hi))
    q_spec = pl.BlockSpec((1, tq, bw), lambda bi, hi, qi: (bi, qi, hi))
    return pl.pallas_call(
        functools.partial(_attn_kernel, tq=tq, lambda_init=lambda_init),
        grid=(b, w // bw, l // tq),
        in_specs=[
            pl.BlockSpec((1, 4, HEAD_DIM), lambda bi, hi, qi: (0, 0, 0)),
            pl.BlockSpec((1, hw), lambda bi, hi, qi: (0, 0)),
            q_spec, kv_spec, kv_spec,
        ],
        out_specs=q_spec,
        out_shape=jax.ShapeDtypeStruct((b, l, w), BF16),
        scratch_shapes=[pltpu.VMEM((n_chains, tq, hw), F32)],
        compiler_params=_params("parallel", "parallel", "arbitrary"),
    )(lam_vecs.reshape(1, 4, HEAD_DIM), subln_g.reshape(1, hw), q, k, v)


def _rope_tables(length, batch):
    inv = 1.0 / (ROPE_THETA ** (jnp.arange(0, HEAD_DIM, 2, dtype=F32) / HEAD_DIM))
    ang = jnp.arange(length, dtype=F32)[:, None] * inv[None, :]
    cos, sin = jnp.cos(ang), jnp.sin(ang)
    cos2 = jnp.concatenate([cos, cos], axis=1)
    sin2 = jnp.concatenate([-sin, sin], axis=1)
    return jnp.tile(cos2, (batch, 1)), jnp.tile(sin2, (batch, 1))


def kernel(x, meta_tokens, a_norm_g, a_w_in, a_conv_w, a_conv_b, a_w_r, a_b_r, a_w_i, a_b_i,
           a_lambda, a_w_out, kv_norm_g, w_kv, b_norm_g, b_w_q, b_lambda, b_subln_g, b_w_o,
           mlp_norm_g, mlp_w1, mlp_w2, final_norm_g):
    batch, seq, d = x.shape
    n_a = a_w_in.shape[0]
    depth = mlp_w1.shape[0]
    d_rnn = a_w_out.shape[1]
    qk_width = b_w_q.shape[2]
    v_width = w_kv.shape[1] - qk_width
    length = N_META_TOKENS + seq
    lp = pl.cdiv(length, SEQ_TILE) * SEQ_TILE
    t = batch * lp
    assert t % TOKEN_TILE == 0

    meta = jnp.broadcast_to(meta_tokens[None].astype(x.dtype), (batch, N_META_TOKENS, d))
    pad = jnp.zeros((batch, lp - length, d), x.dtype)
    h = jnp.concatenate([meta, x, pad], axis=1).reshape(t, d)
    rope = _rope_tables(lp, batch)

    w_in, w_out = a_w_in.astype(BF16), a_w_out.astype(BF16)
    wkv, w_q, w_o = w_kv.astype(BF16)[None], b_w_q.astype(BF16), b_w_o.astype(BF16)
    q_scale = math.log2(math.e) / math.sqrt(HEAD_DIM)

    k_sh = v_sh = None
    for layer in range(depth):
        if layer < n_a:
            j = layer
            gate = _norm_matmul(h, a_norm_g[j], w_in, j, 0, d_rnn, BF16, act="gelu")
            xr = _norm_matmul(h, a_norm_g[j], w_in, j, d_rnn, d_rnn, F32)
            w_g, b_g = _gate_weights(a_w_r[j], a_b_r[j], a_w_i[j], a_b_i[j])
            y = _rglru(xr.reshape(batch, lp, d_rnn), gate.reshape(batch, lp, d_rnn),
                       a_conv_w[j], a_conv_b[j], w_g, b_g, a_lambda[j])
            h = _matmul_residual(y.reshape(t, d_rnn), w_out, j, h)
        else:
            j = layer - n_a
            if j == 0:
                k_sh = _norm_matmul(h, kv_norm_g, wkv, 0, 0, qk_width, BF16, rope=rope)
                v_sh = _norm_matmul(h, kv_norm_g, wkv, 0, qk_width, v_width, BF16)
                k_sh = k_sh.reshape(batch, lp, qk_width)
                v_sh = v_sh.reshape(batch, lp, v_width)
            lambda_init = 0.8 - 0.6 * math.exp(-0.3 * layer)
            q = _norm_matmul(h, b_norm_g[j], w_q, j, 0, qk_width, BF16, rope=rope, scale=q_scale)
            o = _diff_attention(q.reshape(batch, lp, qk_width), k_sh, v_sh, b_lambda[j],
                                b_subln_g[j], lambda_init)
            h = _matmul_residual(o.reshape(t, v_width), w_o, j, h)
        final = (final_norm_g, batch, lp, length) if layer == depth - 1 else None
        h = _mlp(h, mlp_norm_g[layer], mlp_w1, mlp_w2, layer, final)
    return h.reshape(batch, seq, d)
```

```python
import functools
import math

import jax
import jax.numpy as jnp
from jax import lax
from jax.experimental import pallas as pl
from jax.experimental.pallas import tpu as pltpu

F32 = jnp.float32
BF16 = jnp.bfloat16

N_META_TOKENS = 16
CONV_WIDTH = 4
LRU_C = 8.0
HEAD_DIM = 128
ROPE_THETA = 10000.0
NORM_EPS = 1e-6
SUBLN_EPS = 1e-5

LANES = 128
SUBLANES = 8
VMEM_LIMIT_BYTES = 56 * 1024 * 1024

SEQ_TILE = 384
TOKEN_TILE = 768
FF_TILE = 512
MLP_TILE = 1056
GATE_GROUP = 4
SCAN_ROWS = 264
RGLRU_TILE = 1056
KV_BLOCKS_PER_STEP = 2
HEADS_PER_STEP = 2


def _params(*semantics):
    return pltpu.CompilerParams(dimension_semantics=semantics,
                                vmem_limit_bytes=VMEM_LIMIT_BYTES)


def _rms_scale(x, eps):
    return lax.rsqrt(jnp.mean(x * x, axis=-1, keepdims=True) + eps)


def _norm_matmul_kernel(*refs, act, rope, scale):
    if rope:
        x_ref, g_ref, w_ref, cos_ref, sin_ref, o_ref = refs
    else:
        x_ref, g_ref, w_ref, o_ref = refs

    x = x_ref[...]
    xn = (x * _rms_scale(x, NORM_EPS) * g_ref[...]).astype(BF16)
    y = jnp.dot(xn, w_ref[...], preferred_element_type=F32)
    if act == "gelu":
        y = jax.nn.gelu(y)
    if rope:
        c = cos_ref[...]
        s = sin_ref[...]
        segs = []
        for i in range(y.shape[1] // HEAD_DIM):
            seg = y[:, i * HEAD_DIM:(i + 1) * HEAD_DIM]
            segs.append(seg * c + pltpu.roll(seg, HEAD_DIM // 2, axis=1) * s)
        y = jnp.concatenate(segs, axis=1)
    if scale != 1.0:
        y = y * scale
    o_ref[...] = y.astype(o_ref.dtype)


def _norm_matmul(x, g, w, layer, col0, n, out_dtype, act=None, rope=None, scale=1.0):
    t, d = x.shape
    tm = TOKEN_TILE
    assert col0 % n == 0
    j0 = col0 // n
    in_specs = [
        pl.BlockSpec((tm, d), lambda i: (i, 0)),
        pl.BlockSpec((1, d), lambda i: (0, 0)),
        pl.BlockSpec((None, d, n), lambda i: (layer, 0, j0), pipeline_mode=pl.Buffered(1)),
    ]
    args = [x, g.reshape(1, d), w]
    if rope is not None:
        in_specs += [pl.BlockSpec((tm, HEAD_DIM), lambda i: (i, 0))] * 2
        args += list(rope)
    return pl.pallas_call(
        functools.partial(_norm_matmul_kernel, act=act, rope=rope is not None, scale=scale),
        grid=(t // tm,),
        in_specs=in_specs,
        out_specs=pl.BlockSpec((tm, n), lambda i: (i, 0)),
        out_shape=jax.ShapeDtypeStruct((t, n), out_dtype),
        compiler_params=_params("parallel"),
    )(*args)


def _kv_kernel(x_ref, g_ref, w_ref, cos_ref, sin_ref, k_ref, v_ref):
    x = x_ref[...]
    xn = (x * _rms_scale(x, NORM_EPS) * g_ref[...]).astype(BF16)
    nk = k_ref.shape[1]
    c = cos_ref[...]
    s = sin_ref[...]
    yk = jnp.dot(xn, w_ref[:, :nk], preferred_element_type=F32)
    for i in range(nk // HEAD_DIM):
        seg = yk[:, i * HEAD_DIM:(i + 1) * HEAD_DIM]
        k_ref[:, i * HEAD_DIM:(i + 1) * HEAD_DIM] = (
            seg * c + pltpu.roll(seg, HEAD_DIM // 2, axis=1) * s).astype(k_ref.dtype)
    v_ref[...] = jnp.dot(xn, w_ref[:, nk:], preferred_element_type=F32).astype(v_ref.dtype)


def _kv_projection(x, g, w, nk, rope):
    t, d = x.shape
    nv = w.shape[1] - nk
    tm = TOKEN_TILE
    rope_spec = pl.BlockSpec((tm, HEAD_DIM), lambda i: (i, 0))
    return pl.pallas_call(
        _kv_kernel,
        grid=(t // tm,),
        in_specs=[
            pl.BlockSpec((tm, d), lambda i: (i, 0)),
            pl.BlockSpec((1, d), lambda i: (0, 0)),
            pl.BlockSpec((d, nk + nv), lambda i: (0, 0), pipeline_mode=pl.Buffered(1)),
            rope_spec, rope_spec,
        ],
        out_specs=[pl.BlockSpec((tm, nk), lambda i: (i, 0)), pl.BlockSpec((tm, nv), lambda i: (i, 0))],
        out_shape=[jax.ShapeDtypeStruct((t, nk), BF16), jax.ShapeDtypeStruct((t, nv), BF16)],
        compiler_params=_params("parallel"),
    )(x, g.reshape(1, d), w, *rope)


def _matmul_residual_kernel(y_ref, w_ref, r_ref, o_ref):
    o_ref[...] = r_ref[...] + jnp.dot(y_ref[...], w_ref[...], preferred_element_type=F32)


def _matmul_residual(y, w, layer, res):
    t, k = y.shape
    n = w.shape[2]
    tm = TOKEN_TILE
    return pl.pallas_call(
        _matmul_residual_kernel,
        grid=(t // tm,),
        in_specs=[
            pl.BlockSpec((tm, k), lambda i: (i, 0)),
            pl.BlockSpec((None, k, n), lambda i: (layer, 0, 0), pipeline_mode=pl.Buffered(1)),
            pl.BlockSpec((tm, n), lambda i: (i, 0)),
        ],
        out_specs=pl.BlockSpec((tm, n), lambda i: (i, 0)),
        out_shape=jax.ShapeDtypeStruct((t, n), F32),
        compiler_params=_params("parallel"),
    )(y, w, res)


def _mlp_kernel(*refs, segments):
    final = segments is not None
    i = pl.program_id(0)
    j = pl.program_id(1)
    if final:
        (x_hbm, g_ref, w1_ref, w2_ref, fg_ref, out_hbm,
         xn_ref, x_ref, x_sem, o_ref, stage, sems) = refs
    else:
        x_hbm, g_ref, w1_ref, w2_ref, o_ref, xn_ref, x_ref, x_sem = refs
    tm = x_ref.shape[0]

    def x_copy(tile):
        rows = pl.ds(pl.multiple_of(tile * tm, tm), tm)
        return pltpu.make_async_copy(x_hbm.at[rows], x_ref, x_sem.at[0])

    @pl.when((i == 0) & (j == 0))
    def _():
        x_copy(0).start()

    @pl.when(j == 0)
    def _():
        x_copy(i).wait()
        x = x_ref[...]
        xn_ref[...] = (x * _rms_scale(x, NORM_EPS) * g_ref[...]).astype(xn_ref.dtype)
        o_ref[...] = x

    @pl.when((j == 1) & (i + 1 < pl.num_programs(0)))
    def _():
        x_copy(i + 1).start()

    h = jnp.dot(xn_ref[...], w1_ref[...].astype(BF16), preferred_element_type=F32)
    h = jnp.square(jnp.maximum(h, 0.0)).astype(BF16)
    o_ref[...] += jnp.dot(h, w2_ref[...].astype(BF16), preferred_element_type=F32)

    if final:
        def copies(tile):
            return [pltpu.make_async_copy(stage.at[pl.ds(row, count)],
                                          out_hbm.at[pl.ds(dst, count)], sems.at[n])
                    for n, (row, count, dst) in enumerate(segments[tile])]

        def per_tile(action):
            for tile in range(len(segments)):
                pl.when(i == tile)(functools.partial(action, tile))

        def wait_previous(tile):
            if tile > 0:
                for cp in copies(tile - 1):
                    cp.wait()

        def start_current(tile):
            for cp in copies(tile):
                cp.start()
            if tile == len(segments) - 1:
                for cp in copies(tile):
                    cp.wait()

        @pl.when(j == pl.num_programs(1) - 1)
        def _():
            per_tile(wait_previous)
            y = o_ref[...]
            stage[...] = y * _rms_scale(y, NORM_EPS) * fg_ref[...]
            per_tile(start_current)


def _output_segments(n_tiles, tm, lp, length, seq):
    table = []
    for tile in range(n_tiles):
        first = tile * tm
        pieces = []
        for b in range(first // lp, (first + tm - 1) // lp + 1):
            lo = max(first, b * lp + N_META_TOKENS)
            hi = min(first + tm, b * lp + length)
            if hi > lo:
                pieces.append((lo - first, hi - lo, b * seq + lo - b * lp - N_META_TOKENS))
        table.append(tuple(pieces))
    return tuple(table)


def _mlp(x, g, w1, w2, layer, final=None):
    t, d = x.shape
    f = w1.shape[2]
    tm, tf = MLP_TILE, FF_TILE
    assert t % tm == 0 and f // tf >= 2
    grid = (t // tm, f // tf)
    in_specs = [
        pl.BlockSpec(memory_space=pl.ANY),
        pl.BlockSpec((1, d), lambda i, j: (0, 0)),
        pl.BlockSpec((None, d, tf), lambda i, j: (layer, 0, j)),
        pl.BlockSpec((None, tf, d), lambda i, j: (layer, j, 0)),
    ]
    args = [x, g.reshape(1, d), w1, w2]
    scratch_shapes = [pltpu.VMEM((tm, d), BF16), pltpu.VMEM((tm, d), F32),
                      pltpu.SemaphoreType.DMA((1,))]
    params = _params("arbitrary", "arbitrary")
    if final is None:
        return pl.pallas_call(
            functools.partial(_mlp_kernel, segments=None),
            grid=grid,
            in_specs=in_specs,
            out_specs=pl.BlockSpec((tm, d), lambda i, j: (i, 0)),
            out_shape=jax.ShapeDtypeStruct((t, d), F32),
            scratch_shapes=scratch_shapes,
            compiler_params=params,
        )(*args)
    final_g, batch, lp, length = final
    seq = length - N_META_TOKENS
    segments = _output_segments(t // tm, tm, lp, length, seq)
    assert all(row % SUBLANES == 0 and count % SUBLANES == 0 and dst % SUBLANES == 0
               for pieces in segments for row, count, dst in pieces)
    max_pieces = max(len(pieces) for pieces in segments)
    return pl.pallas_call(
        functools.partial(_mlp_kernel, segments=segments),
        grid=grid,
        in_specs=in_specs + [pl.BlockSpec((1, d), lambda i, j: (0, 0))],
        out_specs=pl.BlockSpec(memory_space=pl.ANY),
        out_shape=jax.ShapeDtypeStruct((batch * seq, d), F32),
        scratch_shapes=scratch_shapes + [pltpu.VMEM((tm, d), F32), pltpu.VMEM((tm, d), F32),
                                         pltpu.SemaphoreType.DMA((max_pieces,))],
        compiler_params=params,
    )(*args, final_g.reshape(1, d))


def _rglru_kernel(xr_ref, gate_ref, cw_ref, cb_ref, w_ref, b_ref, lam_ref, y_ref,
                  xl, xc_scr, hl, aend, bend, cin, hcar, *, ts):
    c = xr_ref.shape[2]
    n_lane_tiles = c // LANES
    lane_tiles = range(n_lane_tiles)
    seg = SCAN_ROWS // SUBLANES
    n_groups = ts // SCAN_ROWS
    hdr = SUBLANES

    @pl.when(pl.program_id(2) == 0)
    def _():
        xl[:, 0:hdr, :] = jnp.zeros((n_lane_tiles, hdr, LANES), F32)
        hcar[...] = jnp.zeros(hcar.shape, F32)

    for j in lane_tiles:
        xl[j, hdr:hdr + ts, :] = xr_ref[0, :, j * LANES:(j + 1) * LANES]

    def lane(x, j):
        return x[:, j * LANES:(j + 1) * LANES]

    taps = [cw_ref[tap] for tap in range(CONV_WIDTH)]
    bias = cb_ref[...]
    for g in range(n_groups):
        loads = {}
        for k in range(-(CONV_WIDTH - 1), seg):
            loads[k] = xl[:, pl.ds(hdr + g * SCAN_ROWS + k, SUBLANES, stride=seg), :]
        for k in range(seg):
            xc = loads[k] * taps[CONV_WIDTH - 1] + bias
            for back in range(1, CONV_WIDTH):
                xc = xc + loads[k - back] * taps[CONV_WIDTH - 1 - back]
            r0 = g * SCAN_ROWS + k * SUBLANES
            for j in lane_tiles:
                xc_scr[r0:r0 + SUBLANES, j * LANES:(j + 1) * LANES] = xc[j]
    xl[:, 0:hdr, :] = xl[:, ts:ts + hdr, :]

    xc = xc_scr[...]
    half_gates = jnp.dot(xc.astype(BF16), w_ref[0], preferred_element_type=F32) + b_ref[0]
    t_r = jnp.tanh(half_gates[:, :c])
    t_i = jnp.tanh(half_gates[:, c:])
    half_rate = (-0.5 * LRU_C * math.log2(math.e)) * jax.nn.softplus(-lam_ref[...])
    a = jnp.exp2(t_r * half_rate + half_rate)
    bv = jnp.exp(0.5 * jnp.log(1.0 - a * a)) * ((0.5 * t_i + 0.5) * xc)

    def rows(x, g, k):
        r0 = g * SCAN_ROWS + k * SUBLANES
        return x[r0:r0 + SUBLANES, :]

    for g in range(n_groups):
        a_cum, b_cum = rows(a, g, 0), rows(bv, g, 0)
        for k in range(1, seg):
            a_k = rows(a, g, k)
            b_cum = a_k * b_cum + rows(bv, g, k)
            a_cum = a_k * a_cum
        aend[g * SUBLANES:(g + 1) * SUBLANES, :] = a_cum
        bend[g * SUBLANES:(g + 1) * SUBLANES, :] = b_cum

    carry = hcar[0:1, :]
    for s in range(n_groups * SUBLANES):
        cin[s:s + 1, :] = carry
        carry = aend[s:s + 1, :] * carry + bend[s:s + 1, :]
    hcar[0:1, :] = carry

    for g in range(n_groups):
        h = cin[g * SUBLANES:(g + 1) * SUBLANES, :]
        for k in range(seg):
            h = rows(a, g, k) * h + rows(bv, g, k)
            for j in lane_tiles:
                hl[j, pl.ds(g * SCAN_ROWS + k, SUBLANES, stride=seg), :] = lane(h, j)
    for j in lane_tiles:
        lanes = slice(j * LANES, (j + 1) * LANES)
        y_ref[0, :, lanes] = (hl[j] * gate_ref[0, :, lanes].astype(F32)).astype(y_ref.dtype)


def _rglru(xr, gate, conv_w, conv_b, w_gates, b_gates, lam):
    b, l, c = xr.shape
    groups = w_gates.shape[0]
    cg = c // groups
    nl = cg // LANES
    ts = RGLRU_TILE
    assert l % ts == 0
    n_runs = ts // SCAN_ROWS * SUBLANES
    seq_spec = pl.BlockSpec((1, ts, cg), lambda bi, gi, ci: (bi, ci, gi))
    return pl.pallas_call(
        functools.partial(_rglru_kernel, ts=ts),
        grid=(b, groups, l // ts),
        in_specs=[
            seq_spec, seq_spec,
            pl.BlockSpec((CONV_WIDTH, nl, 1, LANES), lambda bi, gi, ci: (0, gi, 0, 0)),
            pl.BlockSpec((nl, 1, LANES), lambda bi, gi, ci: (gi, 0, 0)),
            pl.BlockSpec((1, cg, 2 * cg), lambda bi, gi, ci: (gi, 0, 0)),
            pl.BlockSpec((1, 1, 2 * cg), lambda bi, gi, ci: (gi, 0, 0)),
            pl.BlockSpec((1, cg), lambda bi, gi, ci: (0, gi)),
        ],
        out_specs=seq_spec,
        out_shape=jax.ShapeDtypeStruct((b, l, c), BF16),
        scratch_shapes=[
            pltpu.VMEM((nl, SUBLANES + ts, LANES), F32),
            pltpu.VMEM((ts, cg), F32),
            pltpu.VMEM((nl, ts, LANES), F32),
            pltpu.VMEM((n_runs, cg), F32),
            pltpu.VMEM((n_runs, cg), F32),
            pltpu.VMEM((n_runs, cg), F32),
            pltpu.VMEM((SUBLANES, cg), F32),
        ],
        compiler_params=_params("parallel", "parallel", "arbitrary"),
    )(xr, gate, conv_w.reshape(CONV_WIDTH, c // LANES, 1, LANES),
      conv_b.reshape(c // LANES, 1, LANES), w_gates, b_gates, lam.reshape(1, c))


def _gate_weights(w_r, b_r, w_i, b_i):
    nb, gb, _ = w_r.shape
    groups = nb // GATE_GROUP
    cg = GATE_GROUP * gb
    block_col = jnp.arange(GATE_GROUP)[:, None, None] * gb + jnp.arange(gb)[None, :, None]
    place = (jnp.arange(cg)[None, None, :] == block_col).astype(BF16)

    def dense(w):
        w = (0.5 * w).astype(BF16).reshape(groups, GATE_GROUP, gb, gb)
        return jnp.einsum("gjab,jbc->gjac", w, place,
                          preferred_element_type=BF16).reshape(groups, cg, cg)

    w = jnp.concatenate([dense(w_r), dense(w_i)], axis=2)
    b = 0.5 * jnp.concatenate([b_r.reshape(groups, 1, cg), b_i.reshape(groups, 1, cg)], axis=2)
    return w, b.astype(F32)


def _attn_kernel(lam_ref, g_ref, q_ref, k_ref, v_ref, o_ref, acc, *, tq, lambda_init):
    qi = pl.program_id(2)
    hw = 2 * HEAD_DIM
    n_chains = acc.shape[0]
    chains = range(n_chains)
    q = q_ref[0]
    q_parts = [q[:, c * HEAD_DIM:(c + 1) * HEAD_DIM] for c in chains]
    acc[...] = jnp.zeros(acc.shape, F32)

    def step(first_block, n_blocks, carry, masked):
        width = n_blocks * tq
        start = pl.multiple_of(first_block * tq, tq)
        k = k_ref[0, pl.ds(start, width), :]
        v = v_ref[0, pl.ds(start, width), :]
        s = [lax.dot_general(q_parts[c], k[:, c * HEAD_DIM:(c + 1) * HEAD_DIM],
                             (((1,), (1,)), ((), ())), preferred_element_type=F32)
             for c in chains]
        if masked:
            row = lax.broadcasted_iota(jnp.int32, s[0].shape, 0)
            col = lax.broadcasted_iota(jnp.int32, s[0].shape, 1)
            keep = col - (n_blocks - 1) * tq <= row
            s = [jnp.where(keep, s[c], -jnp.inf) for c in chains]
        m_new = [jnp.maximum(carry[2 * c], jnp.max(s[c], axis=-1, keepdims=True)) for c in chains]
        p = [jnp.exp2(s[c] - m_new[c]) for c in chains]
        alpha = [jnp.exp2(carry[2 * c] - m_new[c]) for c in chains]
        l_new = [alpha[c] * carry[2 * c + 1] + jnp.sum(p[c], axis=-1, keepdims=True)
                 for c in chains]
        pv = [jnp.dot(p[c].astype(BF16), v[:, (c // 2) * hw:(c // 2 + 1) * hw],
                      preferred_element_type=F32) for c in chains]
        for c in chains:
            acc[c] = alpha[c] * acc[c] + pv[c]
        out = []
        for c in chains:
            out += [m_new[c], l_new[c]]
        return tuple(out)

    neg = jnp.full((tq, 1), -jnp.inf, F32)
    zero = jnp.zeros((tq, 1), F32)
    n_wide = qi // KV_BLOCKS_PER_STEP
    def wide(block, cr):
        return step(block, KV_BLOCKS_PER_STEP, cr, False)

    carry = lax.fori_loop(
        0, n_wide // 2,
        lambda i, cr: wide((2 * i + 1) * KV_BLOCKS_PER_STEP, wide(2 * i * KV_BLOCKS_PER_STEP, cr)),
        (neg, zero) * n_chains)
    carry = lax.cond(n_wide % 2 == 1,
                     lambda cr: wide((n_wide - 1) * KV_BLOCKS_PER_STEP, cr),
                     lambda cr: cr, carry)
    rest = qi - n_wide * KV_BLOCKS_PER_STEP
    tails = [functools.partial(step, qi - r, r + 1, masked=True)
             for r in range(KV_BLOCKS_PER_STEP)]
    carry = lax.switch(rest, tails, carry)

    lv = lam_ref[0]
    lam = (jnp.exp(jnp.sum(lv[0:1] * lv[1:2], axis=-1, keepdims=True))
           - jnp.exp(jnp.sum(lv[2:3] * lv[3:4], axis=-1, keepdims=True)) + lambda_init)
    for head in range(n_chains // 2):
        c0, c1 = 2 * head, 2 * head + 1
        o = acc[c0] * (1.0 / carry[2 * c0 + 1]) - acc[c1] * (lam / carry[2 * c1 + 1])
        o = o * _rms_scale(o, SUBLN_EPS) * g_ref[...] * (1.0 - lambda_init)
        o_ref[0, :, head * hw:(head + 1) * hw] = o.astype(o_ref.dtype)


def _diff_attention(q, k, v, lam_vecs, subln_g, lambda_init):
    b, l, w = q.shape
    hw = 2 * HEAD_DIM
    bw = HEADS_PER_STEP * hw
    n_chains = 2 * HEADS_PER_STEP
    tq = SEQ_TILE
    kv_spec = pl.BlockSpec((1, l, bw), lambda bi, hi, qi: (bi, 0, hi))
    q_spec = pl.BlockSpec((1, tq, bw), lambda bi, hi, qi: (bi, qi, hi))
    return pl.pallas_call(
        functools.partial(_attn_kernel, tq=tq, lambda_init=lambda_init),
        grid=(b, w // bw, l // tq),
        in_specs=[
            pl.BlockSpec((1, 4, HEAD_DIM), lambda bi, hi, qi: (0, 0, 0)),
            pl.BlockSpec((1, hw), lambda bi, hi, qi: (0, 0)),
            q_spec, kv_spec, kv_spec,
        ],
        out_specs=q_spec,
        out_shape=jax.ShapeDtypeStruct((b, l, w), BF16),
        scratch_shapes=[pltpu.VMEM((n_chains, tq, hw), F32)],
        compiler_params=_params("parallel", "parallel", "arbitrary"),
    )(lam_vecs.reshape(1, 4, HEAD_DIM), subln_g.reshape(1, hw), q, k, v)


def _rope_tables(length, batch):
    inv = 1.0 / (ROPE_THETA ** (jnp.arange(0, HEAD_DIM, 2, dtype=F32) / HEAD_DIM))
    ang = jnp.arange(length, dtype=F32)[:, None] * inv[None, :]
    cos, sin = jnp.cos(ang), jnp.sin(ang)
    cos2 = jnp.concatenate([cos, cos], axis=1)
    sin2 = jnp.concatenate([-sin, sin], axis=1)
    return jnp.tile(cos2, (batch, 1)), jnp.tile(sin2, (batch, 1))


def kernel(x, meta_tokens, a_norm_g, a_w_in, a_conv_w, a_conv_b, a_w_r, a_b_r, a_w_i, a_b_i,
           a_lambda, a_w_out, kv_norm_g, w_kv, b_norm_g, b_w_q, b_lambda, b_subln_g, b_w_o,
           mlp_norm_g, mlp_w1, mlp_w2, final_norm_g):
    batch, seq, d = x.shape
    n_a = a_w_in.shape[0]
    depth = mlp_w1.shape[0]
    d_rnn = a_w_out.shape[1]
    qk_width = b_w_q.shape[2]
    v_width = w_kv.shape[1] - qk_width
    length = N_META_TOKENS + seq
    lp = pl.cdiv(length, SEQ_TILE) * SEQ_TILE
    t = batch * lp
    assert t % TOKEN_TILE == 0

    meta = jnp.broadcast_to(meta_tokens[None].astype(x.dtype), (batch, N_META_TOKENS, d))
    pad = jnp.zeros((batch, lp - length, d), x.dtype)
    h = jnp.concatenate([meta, x, pad], axis=1).reshape(t, d)
    rope = _rope_tables(lp, batch)

    w_in, w_out = a_w_in.astype(BF16), a_w_out.astype(BF16)
    wkv, w_q, w_o = w_kv.astype(BF16)[None], b_w_q.astype(BF16), b_w_o.astype(BF16)
    q_scale = math.log2(math.e) / math.sqrt(HEAD_DIM)

    k_sh = v_sh = None
    for layer in range(depth):
        if layer < n_a:
            j = layer
            gate = _norm_matmul(h, a_norm_g[j], w_in, j, 0, d_rnn, BF16, act="gelu")
            xr = _norm_matmul(h, a_norm_g[j], w_in, j, d_rnn, d_rnn, F32)
            w_g, b_g = _gate_weights(a_w_r[j], a_b_r[j], a_w_i[j], a_b_i[j])
            y = _rglru(xr.reshape(batch, lp, d_rnn), gate.reshape(batch, lp, d_rnn),
                       a_conv_w[j], a_conv_b[j], w_g, b_g, a_lambda[j])
            h = _matmul_residual(y.reshape(t, d_rnn), w_out, j, h)
        else:
            j = layer - n_a
            if j == 0:
                k_sh, v_sh = _kv_projection(h, kv_norm_g, wkv[0], qk_width, rope)
                k_sh = k_sh.reshape(batch, lp, qk_width)
                v_sh = v_sh.reshape(batch, lp, v_width)
            lambda_init = 0.8 - 0.6 * math.exp(-0.3 * layer)
            q = _norm_matmul(h, b_norm_g[j], w_q, j, 0, qk_width, BF16, rope=rope, scale=q_scale)
            o = _diff_attention(q.reshape(batch, lp, qk_width), k_sh, v_sh, b_lambda[j],
                                b_subln_g[j], lambda_init)
            h = _matmul_residual(o.reshape(t, v_width), w_o, j, h)
        final = (final_norm_g, batch, lp, length) if layer == depth - 1 else None
        h = _mlp(h, mlp_norm_g[layer], mlp_w1, mlp_w2, layer, final)
    return h.reshape(batch, seq, d)
```
